```python
import math
import jax, jax.numpy as jnp
from jax import lax
import numpy as np

D_MODEL = 1024
BATCH = 2
SEQ = 8192
DEPTH = 2
DEC_BATCH = 32
DEC_SEQ = 32
PAST_LEN = 2048

CHUNK = 64
N_META = 16
QBLOCK = 128
HEAD_DIM = 64
FOX_HEADS = 8
DIFF_HEADS = 4
DIFF_VDIM = 2 * HEAD_DIM
ROPE_THETA = 10000.0
MLA_HEADS = 8
MLA_Q_RANK = 384
MLA_KV_RANK = 256
MLA_NOPE = 128
MLA_ROPE = 64
MLA_VDIM = 128
N_GROUPS = 4
EXPERTS_PER_GROUP = 8
N_EXPERTS = N_GROUPS * EXPERTS_PER_GROUP
TOP_K_IN_GROUP = 2
EXPERT_FF = 256
N_EVEN = (DEPTH + 1) // 2
N_ODD = DEPTH // 2
RMS_EPS = 1e-6
FOX_W = FOX_HEADS * HEAD_DIM
DIFF_QK_W = DIFF_HEADS * 2 * HEAD_DIM
DIFF_V_W = DIFF_HEADS * DIFF_VDIM
EVEN_IN = 3 * FOX_W + FOX_HEADS + 2 * DIFF_QK_W + DIFF_V_W
EVEN_OUT = FOX_W + DIFF_V_W
ODD_IN = MLA_Q_RANK + MLA_KV_RANK + MLA_ROPE
ODD_OUT = MLA_HEADS * MLA_VDIM

kernel_name = 'hybrid_fox_diff_mla_hmoe_stream_step'


def rmsnorm(x, g):
    xf = x.astype(jnp.float32)
    xf = xf * lax.rsqrt(jnp.mean(xf * xf, axis=-1, keepdims=True) + RMS_EPS)
    return xf.astype(x.dtype) * g


def rotary(x, pos):
    d = x.shape[-1]
    inv = ROPE_THETA ** (-jnp.arange(0, d, 2, dtype=jnp.float32) / d)
    ang = pos.astype(jnp.float32)[:, None] * inv[None, :]
    shape = (1, x.shape[1]) + (1,) * (x.ndim - 3) + (d,)
    cos = jnp.concatenate([jnp.cos(ang), jnp.cos(ang)], -1).reshape(shape)
    sin = jnp.concatenate([jnp.sin(ang), jnp.sin(ang)], -1).reshape(shape)
    x1, x2 = jnp.split(x, 2, axis=-1)
    rot = jnp.concatenate([-x2, x1], -1)
    return (x * cos + rot * sin).astype(x.dtype)


def geometry(n_past, n_new):
    meta = jnp.arange(N_META, dtype=jnp.int32)
    meta_chunk = jnp.full((N_META,), -1, jnp.int32)
    frames_q = n_past + jnp.arange(n_new, dtype=jnp.int32)
    frames_k = jnp.arange(n_past + n_new, dtype=jnp.int32)
    pos_q = jnp.concatenate([meta, N_META + frames_q])
    pos_k = jnp.concatenate([meta, N_META + frames_k])
    chunk_q = jnp.concatenate([meta_chunk, frames_q // CHUNK])
    chunk_k = jnp.concatenate([meta_chunk, frames_k // CHUNK])
    return pos_q, pos_k, chunk_q, chunk_k


def with_past(cur, past):
    if past is None:
        return cur
    return jnp.concatenate([cur[:, :N_META], past.astype(cur.dtype), cur[:, N_META:]], axis=1)


def sweep_query_blocks(fn, q_arrays):
    lq = q_arrays[0].shape[1]
    qb = min(QBLOCK, lq)
    nb = -(-lq // qb)
    pad = nb * qb - lq

    def to_blocks(a):
        a = jnp.pad(a, [(0, 0), (0, pad)] + [(0, 0)] * (a.ndim - 2))
        return jnp.moveaxis(a.reshape((a.shape[0], nb, qb) + a.shape[2:]), 1, 0)

    out = lax.map(lambda blk: fn(*blk), tuple(to_blocks(a) for a in q_arrays))
    out = jnp.moveaxis(out, 0, 1)
    return out.reshape((out.shape[0], nb * qb) + out.shape[3:])[:, :lq]


def even_mixer(h, i, l, past, n_past, geo, p):
    pos_q, pos_k, chunk_q, chunk_k = geo
    B, L, _ = h.shape
    proj = h @ p['w_in_even'][i]
    offs = [FOX_W, 2 * FOX_W, 3 * FOX_W, 3 * FOX_W + FOX_HEADS,
            3 * FOX_W + FOX_HEADS + DIFF_QK_W, 3 * FOX_W + FOX_HEADS + 2 * DIFF_QK_W]
    q_f, k_f, v_f, f_lin, q_d, k_d, v_d = jnp.split(proj, offs, axis=-1)
    if past is None:
        pk, pv, plf, pdk, pdv = None, None, None, None, None
    else:
        pk, pv, plf, pdk, pdv = past
    scale = HEAD_DIM ** -0.5

    q_f = q_f.reshape(B, L, FOX_HEADS, HEAD_DIM)
    k_f = k_f.reshape(B, L, FOX_HEADS, HEAD_DIM)
    v_f = v_f.reshape(B, L, FOX_HEADS, HEAD_DIM)
    logf = jax.nn.log_sigmoid((f_lin + p['fox_b_f'][i]).astype(jnp.float32))
    kf_all = with_past(k_f, pk)
    vf_all = with_past(v_f, pv)
    c_k = jnp.cumsum(with_past(logf, plf), axis=1)
    c_q = jnp.concatenate([c_k[:, :N_META], c_k[:, N_META + n_past:]], axis=1)
    c_kT = jnp.moveaxis(c_k, 2, 1)

    def fox_block(q_b, cq_b, pq_b):
        s = jnp.einsum('bqhd,bkhd->bhqk', q_b, kf_all).astype(jnp.float32) * scale
        s = s + jnp.moveaxis(cq_b, 2, 1)[..., None] - c_kT[:, :, None, :]
        mask = pos_k[None, None, None, :] <= pq_b[:, None, :, None]
        pr = jax.nn.softmax(jnp.where(mask, s, -jnp.inf), axis=-1).astype(vf_all.dtype)
        return jnp.einsum('bhqk,bkhd->bqhd', pr, vf_all)

    out_a = sweep_query_blocks(fox_block, [q_f, c_q, pos_q[None]])

    q_d = rotary(q_d.reshape(B, L, DIFF_HEADS, 2, HEAD_DIM), pos_q)
    k_d = rotary(k_d.reshape(B, L, DIFF_HEADS, 2, HEAD_DIM), pos_q)
    v_d = v_d.reshape(B, L, DIFF_HEADS, DIFF_VDIM)
    kd_all = with_past(k_d, pdk)
    vd_all = with_past(v_d, pdv)
    lam_p = p['diff_lambda'][i].astype(jnp.float32)
    lambda_init = 0.8 - 0.6 * math.exp(-0.3 * l)
    lam = jnp.exp(jnp.sum(lam_p[0] * lam_p[1])) - jnp.exp(jnp.sum(lam_p[2] * lam_p[3])) + lambda_init

    def diff_block(q_b, cq_b):
        s = jnp.einsum('bqhmd,bkhmd->bhmqk', q_b, kd_all).astype(jnp.float32) * scale
        mask = chunk_k[None, None, None, None, :] <= cq_b[:, None, None, :, None]
        pr = jax.nn.softmax(jnp.where(mask, s, -jnp.inf), axis=-1)
        diff = (pr[:, :, 0] - lam * pr[:, :, 1]).astype(vd_all.dtype)
        return jnp.einsum('bhqk,bkhe->bqhe', diff, vd_all)

    out_b = sweep_query_blocks(diff_block, [q_d, chunk_q[None]])
    out_b = rmsnorm(out_b, p['diff_subln'][i]) * (1.0 - lambda_init)

    y = jnp.concatenate([out_a.reshape(B, L, FOX_W), out_b.reshape(B, L, DIFF_V_W)], -1) @ p['w_out_even'][i]
    return y, (k_f, v_f, logf, k_d, v_d)


def odd_mixer(h, i, past, n_past, geo, p):
    pos_q, pos_k, chunk_q, chunk_k = geo
    B, L, _ = h.shape
    pc, pr_ = (None, None) if past is None else past
    proj = h @ p['w_in_odd'][i]
    c_q, c_kv, k_r = jnp.split(proj, [MLA_Q_RANK, MLA_Q_RANK + MLA_KV_RANK], axis=-1)
    q = jnp.einsum('blc,chd->blhd', rmsnorm(c_q, p['mla_norm_q'][i]), p['mla_w_uq'][i])
    q_nope, q_rope = jnp.split(q, [MLA_NOPE], axis=-1)
    q_rope = rotary(q_rope, pos_q)
    c_kv = rmsnorm(c_kv, p['mla_norm_kv'][i])
    k_r = rotary(k_r, pos_q)
    ckv_all = with_past(c_kv, pc)
    kr_all = with_past(k_r, pr_)
    q_lat = jnp.einsum('blhn,chn->blhc', q_nope, p['mla_w_uk'][i])
    scale = (MLA_NOPE + MLA_ROPE) ** -0.5

    def mla_block(ql_b, qr_b, cq_b):
        s = (jnp.einsum('bqhc,bkc->bhqk', ql_b, ckv_all)
             + jnp.einsum('bqhr,bkr->bhqk', qr_b, kr_all)).astype(jnp.float32) * scale
        mask = chunk_k[None, None, None, :] <= cq_b[:, None, :, None]
        pr = jax.nn.softmax(jnp.where(mask, s, -jnp.inf), axis=-1).astype(ckv_all.dtype)
        return jnp.einsum('bhqk,bkc->bqhc', pr, ckv_all)

    o_lat = sweep_query_blocks(mla_block, [q_lat, q_rope, chunk_q[None]])
    o = jnp.einsum('blhc,chv->blhv', o_lat, p['mla_w_uv'][i]).reshape(B, L, ODD_OUT)
    return o @ p['w_out_odd'][i], (c_kv, k_r)


def hier_moe(h, l, p):
    B, L, D = h.shape
    t = h.reshape(B * L, D)
    g_logits = (t @ p['moe_w_group'][l] + p['moe_b_group'][l]).astype(jnp.float32)
    g_prob = jax.nn.softmax(g_logits, axis=-1)
    g_sel = jnp.argmax(g_logits, axis=-1)
    p_g = jnp.take_along_axis(g_prob, g_sel[:, None], axis=1)[:, 0]
    e_logits = (t @ p['moe_w_router'][l] + p['moe_b_router'][l]).astype(jnp.float32)
    e_logits = e_logits.reshape(-1, N_GROUPS, EXPERTS_PER_GROUP)
    e_in = jnp.take_along_axis(e_logits, g_sel[:, None, None], axis=1)[:, 0]
    top_v, top_i = lax.top_k(e_in, TOP_K_IN_GROUP)
    w = jax.nn.softmax(top_v, axis=-1) * p_g[:, None]
    expert_id = g_sel[:, None] * EXPERTS_PER_GROUP + top_i
    gate = jnp.sum(jax.nn.one_hot(expert_id, N_EXPERTS, dtype=jnp.float32) * w[..., None], axis=1)
    gate = gate.astype(t.dtype)
    out = jnp.zeros_like(t)
    for g in range(N_GROUPS):
        sl = slice(g * EXPERTS_PER_GROUP, (g + 1) * EXPERTS_PER_GROUP)
        a = jnp.einsum('td,edf->tef', t, p['moe_w_gate'][l, sl])
        b = jnp.einsum('td,edf->tef', t, p['moe_w_up'][l, sl])
        hdn = jax.nn.silu(a) * b * gate[:, sl, None]
        out = out + jnp.einsum('tef,efd->td', hdn, p['moe_w_down'][l, sl])
    return out.reshape(B, L, D)


def trunk(x_frames, past, p):
    B, n_new, D = x_frames.shape
    n_past = 0 if past is None else past['fox_k'].shape[2]
    meta = jnp.broadcast_to(p['meta_tokens'][None].astype(x_frames.dtype), (B, N_META, D))
    x = jnp.concatenate([meta, x_frames], axis=1)
    geo = geometry(n_past, n_new)
    even_keys = ('fox_k', 'fox_v', 'fox_logf', 'diff_k', 'diff_v')
    odd_keys = ('mla_ckv', 'mla_krope')
    rows_even, rows_odd = [], []
    for l in range(DEPTH):
        h = rmsnorm(x, p['norm_mix'][l])
        i = l // 2
        if l % 2 == 0:
            lp = None if past is None else tuple(past[k][i] for k in even_keys)
            y, rows = even_mixer(h, i, l, lp, n_past, geo, p)
            rows_even.append(rows)
        else:
            lp = None if past is None else tuple(past[k][i] for k in odd_keys)
            y, rows = odd_mixer(h, i, lp, n_past, geo, p)
            rows_odd.append(rows)
        x = x + y
        x = x + hier_moe(rmsnorm(x, p['norm_ffn'][l]), l, p)
    y = rmsnorm(x[:, N_META:], p['norm_final'])
    keep = 0 if past is None else N_META
    even_out = tuple(jnp.stack([r[j][:, keep:] for r in rows_even]) for j in range(len(even_keys)))
    odd_out = tuple(jnp.stack([r[j][:, keep:] for r in rows_odd]) for j in range(len(odd_keys)))
    return y, even_out + odd_out


def setup_inputs(seed: int = 0) -> dict:
    key = jax.random.key(seed)
    ks = jax.random.split(key, 40)
    f32 = jnp.float32

    def nrm(k, shape, scale=1.0):
        return jax.random.normal(k, shape, f32) * scale

    return {
        'x_prompt': nrm(ks[0], (BATCH, SEQ, D_MODEL)),
        'x_sample': nrm(ks[1], (DEC_BATCH, DEC_SEQ, D_MODEL)),
        'cache_fox_k': nrm(ks[2], (N_EVEN, DEC_BATCH, PAST_LEN, FOX_HEADS, HEAD_DIM)),
        'cache_fox_v': nrm(ks[3], (N_EVEN, DEC_BATCH, PAST_LEN, FOX_HEADS, HEAD_DIM)),
        'cache_fox_logf': jax.nn.log_sigmoid(2.0 + nrm(ks[4], (N_EVEN, DEC_BATCH, PAST_LEN, FOX_HEADS))),
        'cache_diff_k': nrm(ks[5], (N_EVEN, DEC_BATCH, PAST_LEN, DIFF_HEADS, 2, HEAD_DIM)),
        'cache_diff_v': nrm(ks[6], (N_EVEN, DEC_BATCH, PAST_LEN, DIFF_HEADS, DIFF_VDIM)),
        'cache_mla_ckv': nrm(ks[7], (N_ODD, DEC_BATCH, PAST_LEN, MLA_KV_RANK)),
        'cache_mla_krope': nrm(ks[8], (N_ODD, DEC_BATCH, PAST_LEN, MLA_ROPE)),
        'meta_tokens': nrm(ks[9], (N_META, D_MODEL)),
        'norm_mix': 1.0 + nrm(ks[10], (DEPTH, D_MODEL), 0.01),
        'norm_ffn': 1.0 + nrm(ks[11], (DEPTH, D_MODEL), 0.01),
        'norm_final': 1.0 + nrm(ks[12], (D_MODEL,), 0.01),
        'w_in_even': nrm(ks[13], (N_EVEN, D_MODEL, EVEN_IN), D_MODEL ** -0.5),
        'fox_b_f': 2.0 + nrm(ks[14], (N_EVEN, FOX_HEADS), 0.1),
        'diff_lambda': nrm(ks[15], (N_EVEN, 4, HEAD_DIM), 0.1),
        'diff_subln': 1.0 + nrm(ks[16], (N_EVEN, DIFF_VDIM), 0.01),
        'w_out_even': nrm(ks[17], (N_EVEN, EVEN_OUT, D_MODEL), EVEN_OUT ** -0.5),
        'w_in_odd': nrm(ks[18], (N_ODD, D_MODEL, ODD_IN), D_MODEL ** -0.5),
        'mla_norm_q': 1.0 + nrm(ks[19], (N_ODD, MLA_Q_RANK), 0.01),
        'mla_norm_kv': 1.0 + nrm(ks[20], (N_ODD, MLA_KV_RANK), 0.01),
        'mla_w_uq': nrm(ks[21], (N_ODD, MLA_Q_RANK, MLA_HEADS, MLA_NOPE + MLA_ROPE), MLA_Q_RANK ** -0.5),
        'mla_w_uk': nrm(ks[22], (N_ODD, MLA_KV_RANK, MLA_HEADS, MLA_NOPE), MLA_KV_RANK ** -0.5),
        'mla_w_uv': nrm(ks[23], (N_ODD, MLA_KV_RANK, MLA_HEADS, MLA_VDIM), MLA_KV_RANK ** -0.5),
        'w_out_odd': nrm(ks[24], (N_ODD, ODD_OUT, D_MODEL), ODD_OUT ** -0.5),
        'moe_w_group': nrm(ks[25], (DEPTH, D_MODEL, N_GROUPS), D_MODEL ** -0.5),
        'moe_b_group': nrm(ks[26], (DEPTH, N_GROUPS), 0.01),
        'moe_w_router': nrm(ks[27], (DEPTH, D_MODEL, N_EXPERTS), D_MODEL ** -0.5),
        'moe_b_router': nrm(ks[28], (DEPTH, N_EXPERTS), 0.01),
        'moe_w_gate': nrm(ks[29], (DEPTH, N_EXPERTS, D_MODEL, EXPERT_FF), D_MODEL ** -0.5),
        'moe_w_up': nrm(ks[30], (DEPTH, N_EXPERTS, D_MODEL, EXPERT_FF), D_MODEL ** -0.5),
        'moe_w_down': nrm(ks[31], (DEPTH, N_EXPERTS, EXPERT_FF, D_MODEL), EXPERT_FF ** -0.5),
    }


def reference(x_prompt, x_sample, cache_fox_k, cache_fox_v, cache_fox_logf, cache_diff_k, cache_diff_v,
              cache_mla_ckv, cache_mla_krope, meta_tokens, norm_mix, norm_ffn, norm_final,
              w_in_even, fox_b_f, diff_lambda, diff_subln, w_out_even,
              w_in_odd, mla_norm_q, mla_norm_kv, mla_w_uq, mla_w_uk, mla_w_uv, w_out_odd,
              moe_w_group, moe_b_group, moe_w_router, moe_b_router, moe_w_gate, moe_w_up, moe_w_down):
    p = dict(meta_tokens=meta_tokens, norm_mix=norm_mix, norm_ffn=norm_ffn, norm_final=norm_final,
             w_in_even=w_in_even, fox_b_f=fox_b_f, diff_lambda=diff_lambda, diff_subln=diff_subln,
             w_out_even=w_out_even, w_in_odd=w_in_odd, mla_norm_q=mla_norm_q, mla_norm_kv=mla_norm_kv,
             mla_w_uq=mla_w_uq, mla_w_uk=mla_w_uk, mla_w_uv=mla_w_uv, w_out_odd=w_out_odd,
             moe_w_group=moe_w_group, moe_b_group=moe_b_group, moe_w_router=moe_w_router,
             moe_b_router=moe_b_router, moe_w_gate=moe_w_gate, moe_w_up=moe_w_up, moe_w_down=moe_w_down)
    past = dict(fox_k=cache_fox_k, fox_v=cache_fox_v, fox_logf=cache_fox_logf, diff_k=cache_diff_k,
                diff_v=cache_diff_v, mla_ckv=cache_mla_ckv, mla_krope=cache_mla_krope)
    y_prompt, st_p = trunk(x_prompt, None, p)
    y_sample, st_s = trunk(x_sample, past, p)
    fox_k_p, fox_v_p, fox_logf_p, diff_k_p, diff_v_p, mla_ckv_p, mla_krope_p = st_p
    fox_k_s, fox_v_s, fox_logf_s, diff_k_s, diff_v_s, mla_ckv_s, mla_krope_s = st_s
    return (y_prompt, y_sample,
            fox_k_p, fox_v_p, fox_logf_p, diff_k_p, diff_v_p, mla_ckv_p, mla_krope_p,
            fox_k_s, fox_v_s, fox_logf_s, diff_k_s, diff_v_s, mla_ckv_s, mla_krope_s)
```

```python
import functools
import math

import numpy as np
import jax
import jax.numpy as jnp
from jax import lax
from jax.experimental import pallas as pl
from jax.experimental.pallas import tpu as pltpu

F32 = jnp.float32
BF16 = jnp.bfloat16
HIGHEST = lax.Precision.HIGHEST

CHUNK = 64
N_META = 16
HEAD_DIM = 64
FOX_HEADS = 8
DIFF_HEADS = 4
DIFF_VDIM = 128
ROPE_THETA = 10000.0
MLA_HEADS = 8
MLA_Q_RANK = 384
MLA_KV_RANK = 256
MLA_NOPE = 128
MLA_ROPE = 64
MLA_VDIM = 128
N_GROUPS = 4
EXPERTS_PER_GROUP = 8
N_EXPERTS = N_GROUPS * EXPERTS_PER_GROUP
EXPERT_FF = 256
RMS_EPS = 1e-6
FOX_W = FOX_HEADS * HEAD_DIM
MLA_QW = MLA_KV_RANK + 128

LANES = 128
TM = 256
TQ = 256
ROW_CHUNK = 16
LOCAL_CHUNKS = 21
LOCAL_ROWS = LOCAL_CHUNKS * ROW_CHUNK
LOCAL_ROWS_PAD = 384
EXPERT_TILE_CHUNKS = TM // ROW_CHUNK
NEG = -1e30
VMEM_LIMIT = 56 * 1024 * 1024


def _cparams(sem):
    return pltpu.CompilerParams(dimension_semantics=sem, vmem_limit_bytes=VMEM_LIMIT)


def _rms(x, g):
    ms = jnp.mean(x * x, axis=-1, keepdims=True)
    return (x * lax.rsqrt(ms + RMS_EPS)) * g


def _nt_dot(a, b):
    return lax.dot_general(a, b, (((1,), (1,)), ((), ())), preferred_element_type=F32)


def _dot(a, b):
    return jnp.dot(a, b, preferred_element_type=F32)


def _proj_even_kernel(x_ref, g_ref, w_ref, bf_ref, cs_ref, sn_ref,
                      qf_ref, kf_ref, vf_ref, qd_ref, kd_ref, vd_ref,
                      kf32_ref, vf32_ref, lf_ref, kd32_ref, vd32_ref):
    h = _rms(x_ref[...], g_ref[...]).astype(BF16)

    def mm(i, width=FOX_W):
        return _dot(h, w_ref[:, i:i + width])

    scale = HEAD_DIM ** -0.5
    qf = mm(0)
    kf = mm(512)
    vf = mm(1024)
    fl = mm(1536, LANES)
    qd = mm(1664)
    qdr = mm(2176)
    kd = mm(2688)
    kdr = mm(3200)
    vd = mm(3712)
    cos = jnp.concatenate([cs_ref[...]] * 4, axis=1)
    sin = jnp.concatenate([sn_ref[...]] * 4, axis=1)
    qd = qd * cos + qdr * sin
    kd = kd * cos + kdr * sin
    z = fl + bf_ref[...]
    lf_ref[...] = jnp.minimum(z, 0.0) - jnp.log1p(jnp.exp(-jnp.abs(z)))
    qf_ref[...] = (qf * scale).astype(BF16)
    kf_ref[...] = kf.astype(BF16)
    vf_ref[...] = vf.astype(BF16)
    qd_ref[...] = (qd * scale).astype(BF16)
    kd_ref[...] = kd.astype(BF16)
    vd_ref[...] = vd.astype(BF16)
    kf32_ref[...] = kf
    vf32_ref[...] = vf
    kd32_ref[...] = kd
    vd32_ref[...] = vd


def _proj_even(x, g, w_all, b_f, cos2, sin2):
    T = x.shape[0]
    row = lambda w: pl.BlockSpec((TM, w), lambda i: (i, 0))
    full = lambda a: pl.BlockSpec(a.shape, lambda i: (0,) * a.ndim)
    outs = ([jax.ShapeDtypeStruct((T, FOX_W), BF16)] * 6
            + [jax.ShapeDtypeStruct((T, FOX_W), F32)] * 2
            + [jax.ShapeDtypeStruct((T, LANES), F32)]
            + [jax.ShapeDtypeStruct((T, FOX_W), F32)] * 2)
    out_specs = [row(FOX_W)] * 8 + [row(LANES)] + [row(FOX_W)] * 2
    return pl.pallas_call(
        _proj_even_kernel,
        grid=(T // TM,),
        in_specs=[row(x.shape[1]), full(g), full(w_all), full(b_f), row(LANES), row(LANES)],
        out_specs=out_specs,
        out_shape=outs,
        compiler_params=_cparams(("parallel",)),
        name="proj_even",
    )(x, g, w_all, b_f, cos2, sin2)


def _proj_odd_kernel(x_ref, g_ref, w_ref, gq_ref, gkv_ref, wuq_ref, wuk_ref, cs_ref, sn_ref,
                     q_ref, kcat_ref, ckv32_ref, kr32_ref):
    h = _rms(x_ref[...], g_ref[...]).astype(BF16)
    scale = (MLA_NOPE + MLA_ROPE) ** -0.5
    cq = _dot(h, w_ref[:, 0:MLA_Q_RANK])
    ckv = _dot(h, w_ref[:, MLA_Q_RANK:MLA_Q_RANK + MLA_KV_RANK])
    kr = _dot(h, w_ref[:, 640:768])
    krr = _dot(h, w_ref[:, 768:896])
    cos = cs_ref[...]
    sin = sn_ref[...]
    kr = kr * cos + krr * sin
    ckv = _rms(ckv, gkv_ref[...])
    ckv32_ref[...] = ckv
    kr32_ref[...] = kr
    kcat_ref[...] = jnp.concatenate([ckv, kr], axis=1).astype(BF16)
    cqn = _rms(cq, gq_ref[...]).astype(BF16)
    hw = MLA_HEADS * LANES
    q_nope = _dot(cqn, wuq_ref[:, 0:hw])
    q_rope = _dot(cqn, wuq_ref[:, hw:2 * hw])
    q_rope_r = _dot(cqn, wuq_ref[:, 2 * hw:3 * hw])
    cos8 = jnp.concatenate([cos] * MLA_HEADS, axis=1)
    sin8 = jnp.concatenate([sin] * MLA_HEADS, axis=1)
    q_rope = (q_rope * cos8 + q_rope_r * sin8) * scale
    pieces = []
    for hd in range(MLA_HEADS):
        qn = q_nope[:, hd * LANES:(hd + 1) * LANES].astype(BF16)
        pieces.append((_dot(qn, wuk_ref[hd]) * scale).astype(BF16))
        pieces.append(q_rope[:, hd * LANES:(hd + 1) * LANES].astype(BF16))
    q_ref[...] = jnp.concatenate(pieces, axis=1)


def _proj_odd(x, g, w_all, gq, gkv, wuq, wuk, cosp, sinp):
    T = x.shape[0]
    row = lambda w: pl.BlockSpec((TM, w), lambda i: (i, 0))
    full = lambda a: pl.BlockSpec(a.shape, lambda i: (0,) * a.ndim)
    outs = [jax.ShapeDtypeStruct((T, MLA_HEADS * MLA_QW), BF16),
            jax.ShapeDtypeStruct((T, MLA_QW), BF16),
            jax.ShapeDtypeStruct((T, MLA_KV_RANK), F32),
            jax.ShapeDtypeStruct((T, LANES), F32)]
    return pl.pallas_call(
        _proj_odd_kernel,
        grid=(T // TM,),
        in_specs=[row(x.shape[1]), full(g), full(w_all), full(gq), full(gkv), full(wuq), full(wuk),
                  row(LANES), row(LANES)],
        out_specs=[row(MLA_HEADS * MLA_QW), row(MLA_QW), row(MLA_KV_RANK), row(LANES)],
        out_shape=outs,
        compiler_params=_cparams(("parallel",)),
        name="proj_odd",
    )(x, g, w_all, gq, gkv, wuq, wuk, cosp, sinp)


def _cumsum_kernel(x_ref, o_ref):
    rows, length = x_ref.shape
    r = lax.broadcasted_iota(jnp.int32, (LANES, LANES), 0)
    c = lax.broadcasted_iota(jnp.int32, (LANES, LANES), 1)
    tri = jnp.where(r <= c, 1.0, 0.0).astype(BF16)

    def body(i, carry):
        off = pl.multiple_of(i * LANES, LANES)
        blk = x_ref[:, pl.ds(off, LANES)]
        hi = blk.astype(BF16)
        r1 = blk - hi.astype(F32)
        mid = r1.astype(BF16)
        lo = (r1 - mid.astype(F32)).astype(BF16)
        cs = (_dot(hi, tri) + _dot(mid, tri)) + _dot(lo, tri) + carry
        o_ref[:, pl.ds(off, LANES)] = cs
        return cs[:, LANES - 1:LANES]

    lax.fori_loop(0, length // LANES, body, jnp.zeros((rows, 1), F32))


def _cumsum_rows(x):
    pad = (-x.shape[1]) % LANES
    xp = jnp.pad(x, ((0, 0), (0, pad)))
    return pl.pallas_call(
        _cumsum_kernel,
        out_shape=jax.ShapeDtypeStruct(xp.shape, F32),
        compiler_params=pltpu.CompilerParams(vmem_limit_bytes=VMEM_LIMIT),
        name="logf_cumsum",
    )(xp)


def _softmax_step(s, m, l, v, rowc=None):
    smax = jnp.max(s, axis=1, keepdims=True)
    if rowc is not None:
        smax = smax + rowc
    m_new = jnp.maximum(m, smax)
    shift = m_new if rowc is None else m_new - rowc
    alpha = jnp.exp(m - m_new)
    p = jnp.exp(s - shift)
    l_new = alpha * l + jnp.sum(p, axis=1, keepdims=True)
    pv = _dot(p.astype(BF16), v)
    return m_new, l_new, alpha, pv


def _block_iotas(rows, cols):
    r = lax.broadcasted_iota(jnp.int32, (rows, cols), 0)
    c = lax.broadcasted_iota(jnp.int32, (rows, cols), 1)
    return r, c


def _frame_blocks(qq, nqb, n_prompt_blocks):
    tail = qq >= n_prompt_blocks
    return jnp.where(tail, 0, qq % nqb), jnp.where(tail, 0, qq % nqb + 1)


def _prompt_batch(qq, nqb, n_batch):
    return jnp.minimum(qq // nqb, n_batch - 1)


def _fox_prompt_kernel(q_ref, k_ref, v_ref, km_ref, vm_ref, cq_ref, ck_ref, ckm_ref, o_ref, *, nqb, n_prompt_blocks):
    qq = pl.program_id(1)
    is_meta = qq == pl.num_programs(1) - 1
    n_full, n_all = _frame_blocks(qq, nqb, n_prompt_blocks)
    q = q_ref[...]
    lane = lax.broadcasted_iota(jnp.int32, (TQ, LANES), 1)
    lo = lane < HEAD_DIM
    zero = jnp.zeros_like(q)
    qh = (jnp.where(lo, q, zero), jnp.where(lo, zero, q))
    cq = cq_ref[0]
    cqh = (cq[:, 0:1], cq[:, 1:2])

    def merge(carry, upd):
        (m0, l0, m1, l1, acc) = carry
        (m0n, l0n, a0, pv0), (m1n, l1n, a1, pv1) = upd
        acc = acc * jnp.where(lo, a0, a1) + jnp.where(lo, pv0, pv1)
        return (m0n, l0n, m1n, l1n, acc)

    init = (jnp.full((TQ, 1), NEG, F32), jnp.zeros((TQ, 1), F32),
            jnp.full((TQ, 1), NEG, F32), jnp.zeros((TQ, 1), F32),
            jnp.zeros((TQ, LANES), F32))

    km = km_ref[...]
    vm = vm_ref[...]
    r16, c16 = _block_iotas(TQ, N_META)
    hide = jnp.logical_and(is_meta, c16 > r16)
    upd = []
    for hh in range(2):
        s = _nt_dot(qh[hh], km) - ckm_ref[0, hh:hh + 1, :]
        s = jnp.where(hide, NEG, s)
        upd.append(_softmax_step(s, init[2 * hh], init[2 * hh + 1], vm, cqh[hh]))
    carry = merge(init, upd)

    rr, cc = _block_iotas(TQ, TQ)
    causal = cc <= rr

    def step(kb, carry, masked):
        off = pl.multiple_of(kb * TQ, TQ)
        kblk = k_ref[pl.ds(off, TQ), :]
        vblk = v_ref[pl.ds(off, TQ), :]
        upd = []
        for hh in range(2):
            s = _nt_dot(qh[hh], kblk) - ck_ref[0, 0, hh:hh + 1, pl.ds(off, TQ)]
            if masked:
                s = jnp.where(causal, s, NEG)
            upd.append(_softmax_step(s, carry[2 * hh], carry[2 * hh + 1], vblk, cqh[hh]))
        return merge(carry, upd)

    carry = lax.fori_loop(0, n_full, lambda kb, c: step(kb, c, False), carry)
    carry = lax.fori_loop(n_full, n_all, lambda kb, c: step(kb, c, True), carry)
    (m0, l0, m1, l1, acc) = carry
    o_ref[...] = (acc / jnp.where(lo, l0, l1)).astype(o_ref.dtype)


def _fox_prompt(qf, kf, vf, c_col, c_row, c_row_meta, geo):
    T = qf.shape[0]
    nqb = geo.lp // TQ
    npb = geo.bp * nqb
    qmap = lambda g, qq: (qq, g)
    kvmap = lambda g, qq: (_prompt_batch(qq, nqb, geo.bp), g)
    mmap = lambda g, qq: (geo.meta0 // N_META, g)
    kern = functools.partial(_fox_prompt_kernel, nqb=nqb, n_prompt_blocks=npb)
    return pl.pallas_call(
        kern,
        grid=(FOX_HEADS // 2, T // TQ),
        in_specs=[pl.BlockSpec((TQ, LANES), qmap),
                  pl.BlockSpec((geo.lp, LANES), kvmap),
                  pl.BlockSpec((geo.lp, LANES), kvmap),
                  pl.BlockSpec((N_META, LANES), mmap),
                  pl.BlockSpec((N_META, LANES), mmap),
                  pl.BlockSpec((1, TQ, 2), lambda g, qq: (g, qq, 0)),
                  pl.BlockSpec((1, 1, 2, geo.lp), lambda g, qq: (_prompt_batch(qq, nqb, geo.bp), g, 0, 0)),
                  pl.BlockSpec((1, 2, N_META), lambda g, qq: (g, 0, 0))],
        out_specs=pl.BlockSpec((TQ, LANES), qmap),
        out_shape=jax.ShapeDtypeStruct((T, FOX_W), BF16),
        compiler_params=_cparams(("parallel", "arbitrary")),
        name="fox_prompt",
    )(qf, kf, vf, kf, vf, c_col, c_row, c_row_meta)


def _diff_lambda(lam_ref, lambda_init):
    lp = lam_ref[...]
    a = jnp.sum(lp[0:1, :] * lp[1:2, :], axis=1, keepdims=True)
    b = jnp.sum(lp[2:3, :] * lp[3:4, :], axis=1, keepdims=True)
    return jnp.exp(a) - jnp.exp(b) + lambda_init


def _diff_prompt_kernel(q_ref, k_ref, v_ref, km_ref, vm_ref, lam_ref, sub_ref, o_ref, *,
                        nqb, n_prompt_blocks, lambda_init):
    qq = pl.program_id(1)
    n_full, n_all = _frame_blocks(qq, nqb, n_prompt_blocks)
    q = q_ref[...]
    lane = lax.broadcasted_iota(jnp.int32, (TQ, LANES), 1)
    lo = lane < HEAD_DIM
    zero = jnp.zeros_like(q)
    qh = (jnp.where(lo, q, zero), jnp.where(lo, zero, q))

    init = (jnp.full((TQ, 1), NEG, F32), jnp.zeros((TQ, 1), F32), jnp.zeros((TQ, LANES), F32)) * 2

    def update(carry, kblk, vblk, mask):
        out = []
        for hh in range(2):
            m, l, acc = carry[3 * hh:3 * hh + 3]
            s = _nt_dot(qh[hh], kblk)
            if mask is not None:
                s = jnp.where(mask, s, NEG)
            m, l, alpha, pv = _softmax_step(s, m, l, vblk)
            out += [m, l, acc * alpha + pv]
        return tuple(out)

    carry = update(init, km_ref[...], vm_ref[...], None)

    rr, cc = _block_iotas(TQ, TQ)
    chunk_ok = (cc // CHUNK) <= (rr // CHUNK)

    def step(kb, carry, masked):
        off = pl.multiple_of(kb * TQ, TQ)
        return update(carry, k_ref[pl.ds(off, TQ), :], v_ref[pl.ds(off, TQ), :], chunk_ok if masked else None)

    carry = lax.fori_loop(0, n_full, lambda kb, c: step(kb, c, False), carry)
    carry = lax.fori_loop(n_full, n_all, lambda kb, c: step(kb, c, True), carry)
    (m0, l0, acc0, m1, l1, acc1) = carry
    lam = _diff_lambda(lam_ref, lambda_init)
    o = acc0 / l0 - lam * (acc1 / l1)
    o = _rms(o, sub_ref[...]) * (1.0 - lambda_init)
    o_ref[...] = o.astype(o_ref.dtype)


def _diff_prompt(qd, kd, vd, lam_p, subln, geo, lambda_init):
    T = qd.shape[0]
    nqb = geo.lp // TQ
    npb = geo.bp * nqb
    qmap = lambda g, qq: (qq, g)
    kvmap = lambda g, qq: (_prompt_batch(qq, nqb, geo.bp), g)
    mmap = lambda g, qq: (geo.meta0 // N_META, g)
    full = lambda a: pl.BlockSpec(a.shape, lambda g, qq: (0,) * a.ndim)
    kern = functools.partial(_diff_prompt_kernel, nqb=nqb, n_prompt_blocks=npb, lambda_init=lambda_init)
    return pl.pallas_call(
        kern,
        grid=(DIFF_HEADS, T // TQ),
        in_specs=[pl.BlockSpec((TQ, LANES), qmap),
                  pl.BlockSpec((geo.lp, LANES), kvmap),
                  pl.BlockSpec((geo.lp, LANES), kvmap),
                  pl.BlockSpec((N_META, LANES), mmap),
                  pl.BlockSpec((N_META, LANES), mmap),
                  full(lam_p), full(subln)],
        out_specs=pl.BlockSpec((TQ, LANES), qmap),
        out_shape=jax.ShapeDtypeStruct((T, DIFF_HEADS * DIFF_VDIM), BF16),
        compiler_params=_cparams(("parallel", "arbitrary")),
        name="diff_prompt",
    )(qd, kd, vd, kd, vd, lam_p, subln)


def _mla_prompt_kernel(q_ref, k_ref, km_ref, wuv_ref, o_ref, *, nqb, n_prompt_blocks):
    qq = pl.program_id(0)
    n_full, n_all = _frame_blocks(qq, nqb, n_prompt_blocks)
    rr, cc = _block_iotas(TQ, TQ)
    chunk_ok = (cc // CHUNK) <= (rr // CHUNK)
    km = km_ref[...]

    for hd in range(MLA_HEADS):
        qh = q_ref[:, hd * MLA_QW:(hd + 1) * MLA_QW]

        def update(carry, kblk, mask):
            m, l, acc = carry
            s = _nt_dot(qh, kblk)
            if mask is not None:
                s = jnp.where(mask, s, NEG)
            m, l, alpha, pv = _softmax_step(s, m, l, kblk[:, 0:MLA_KV_RANK])
            return (m, l, acc * alpha + pv)

        def step(kb, carry, masked):
            off = pl.multiple_of(kb * TQ, TQ)
            return update(carry, k_ref[pl.ds(off, TQ), :], chunk_ok if masked else None)

        carry = (jnp.full((TQ, 1), NEG, F32), jnp.zeros((TQ, 1), F32), jnp.zeros((TQ, MLA_KV_RANK), F32))
        carry = update(carry, km, None)
        carry = lax.fori_loop(0, n_full, lambda kb, c: step(kb, c, False), carry)
        carry = lax.fori_loop(n_full, n_all, lambda kb, c: step(kb, c, True), carry)
        m, l, acc = carry
        o = _dot((acc / l).astype(BF16), wuv_ref[hd])
        o_ref[:, hd * MLA_VDIM:(hd + 1) * MLA_VDIM] = o.astype(o_ref.dtype)


def _mla_prompt(qcat, kcat, wuv, geo):
    T = qcat.shape[0]
    nqb = geo.lp // TQ
    npb = geo.bp * nqb
    kern = functools.partial(_mla_prompt_kernel, nqb=nqb, n_prompt_blocks=npb)
    return pl.pallas_call(
        kern,
        grid=(T // TQ,),
        in_specs=[pl.BlockSpec((TQ, MLA_HEADS * MLA_QW), lambda qq: (qq, 0)),
                  pl.BlockSpec((geo.lp, MLA_QW), lambda qq: (_prompt_batch(qq, nqb, geo.bp), 0)),
                  pl.BlockSpec((N_META, MLA_QW), lambda qq: (geo.meta0 // N_META, 0)),
                  pl.BlockSpec(wuv.shape, lambda qq: (0, 0, 0))],
        out_specs=pl.BlockSpec((TQ, MLA_HEADS * MLA_VDIM), lambda qq: (qq, 0)),
        out_shape=jax.ShapeDtypeStruct((T, MLA_HEADS * MLA_VDIM), BF16),
        compiler_params=_cparams(("arbitrary",)),
        name="mla_prompt",
    )(qcat, kcat, kcat, wuv)


def _block_diag_queries(q, n_blocks, width):
    ls = q.shape[0]
    lane = lax.broadcasted_iota(jnp.int32, q.shape, 1)
    zero = jnp.zeros_like(q)
    return jnp.concatenate([jnp.where(lane // width == r, q, zero) for r in range(n_blocks)], axis=0)


def _expand_rows(c, ls):
    return jnp.concatenate([jnp.broadcast_to(c[r:r + 1, :], (ls, c.shape[1])) for r in range(c.shape[0])], axis=0)


def _three_part_softmax(parts):
    m = parts[0][0].max(axis=1, keepdims=True)
    for s, _ in parts[1:]:
        m = jnp.maximum(m, s.max(axis=1, keepdims=True))
    l = 0.0
    acc = 0.0
    for s, v in parts:
        p = jnp.exp(s - m)
        l = l + jnp.sum(p, axis=1, keepdims=True)
        acc = acc + _dot(p.astype(BF16), v)
    return acc / l


def _fox_sample_kernel(q_ref, kn_ref, vn_ref, km_ref, vm_ref, kp_ref, vp_ref,
                       cq_ref, ckm_ref, ckp_ref, ckn_ref, prev_ref, o_ref):
    del prev_ref
    ls = q_ref.shape[0]
    qbd = _block_diag_queries(q_ref[...], FOX_HEADS, HEAD_DIM)
    cq = cq_ref[...]
    rowc = jnp.concatenate([cq[:, r:r + 1] for r in range(FOX_HEADS)], axis=0)
    s_meta = _nt_dot(qbd, km_ref[...]) - _expand_rows(ckm_ref[...], ls) + rowc
    s_past = _nt_dot(qbd, kp_ref[0].astype(BF16)) - _expand_rows(ckp_ref[0], ls) + rowc
    s_new = _nt_dot(qbd, kn_ref[...]) - _expand_rows(ckn_ref[0], ls) + rowc
    rr, cc = _block_iotas(FOX_HEADS * ls, ls)
    s_new = jnp.where(cc <= rr % ls, s_new, NEG)
    obd = _three_part_softmax([(s_meta, vm_ref[...]), (s_past, vp_ref[0].astype(BF16)), (s_new, vn_ref[...])])
    lane = lax.broadcasted_iota(jnp.int32, (ls, FOX_W), 1)
    out = jnp.zeros((ls, FOX_W), F32)
    for r in range(FOX_HEADS):
        out = out + jnp.where(lane // HEAD_DIM == r, obd[r * ls:(r + 1) * ls, :], 0.0)
    o_ref[...] = out.astype(o_ref.dtype)


def _fox_sample(qf, kf, vf, past_k, past_v, c_col8, c_row_meta8, c_row_past, c_row_new, prev, geo):
    ls, bs, P = geo.ls, geo.bs, geo.past
    row0 = geo.sample0 // ls
    rmap = lambda s: (row0 + s, 0)
    mmap = lambda s: (geo.meta0 // N_META, 0)
    return pl.pallas_call(
        _fox_sample_kernel,
        grid=(bs,),
        in_specs=[pl.BlockSpec((ls, FOX_W), rmap), pl.BlockSpec((ls, FOX_W), rmap), pl.BlockSpec((ls, FOX_W), rmap),
                  pl.BlockSpec((N_META, FOX_W), mmap), pl.BlockSpec((N_META, FOX_W), mmap),
                  pl.BlockSpec((1, P, FOX_W), lambda s: (s, 0, 0)), pl.BlockSpec((1, P, FOX_W), lambda s: (s, 0, 0)),
                  pl.BlockSpec((ls, FOX_HEADS), rmap),
                  pl.BlockSpec((FOX_HEADS, N_META), lambda s: (0, 0)),
                  pl.BlockSpec((1, FOX_HEADS, P), lambda s: (s, 0, 0)),
                  pl.BlockSpec((1, FOX_HEADS, ls), lambda s: (s, 0, 0)),
                  pl.BlockSpec(memory_space=pl.ANY)],
        out_specs=pl.BlockSpec((ls, FOX_W), rmap),
        out_shape=jax.ShapeDtypeStruct(prev.shape, prev.dtype),
        input_output_aliases={11: 0},
        compiler_params=_cparams(("parallel",)),
        name="fox_sample",
    )(qf, kf, vf, kf, vf, past_k, past_v, c_col8, c_row_meta8, c_row_past, c_row_new, prev)


def _diff_sample_kernel(q_ref, kn_ref, vn_ref, km_ref, vm_ref, kp_ref, vp_ref, lam_ref, sub_ref, prev_ref, o_ref, *,
                        lambda_init, past):
    del prev_ref
    ls = q_ref.shape[0]
    nb = 2 * DIFF_HEADS
    qbd = _block_diag_queries(q_ref[...], nb, HEAD_DIM)
    s_meta = _nt_dot(qbd, km_ref[...])
    s_past = _nt_dot(qbd, kp_ref[0].astype(BF16))
    s_new = _nt_dot(qbd, kn_ref[...])
    rr, cc = _block_iotas(nb * ls, ls)
    s_new = jnp.where((past + cc) // CHUNK <= (past + rr % ls) // CHUNK, s_new, NEG)
    obd = _three_part_softmax([(s_meta, vm_ref[...]), (s_past, vp_ref[0].astype(BF16)), (s_new, vn_ref[...])])
    width = DIFF_HEADS * DIFF_VDIM
    lane = lax.broadcasted_iota(jnp.int32, (ls, width), 1)
    o0 = jnp.zeros((ls, width), F32)
    o1 = jnp.zeros((ls, width), F32)
    for hd in range(DIFF_HEADS):
        sel = lane // DIFF_VDIM == hd
        o0 = o0 + jnp.where(sel, obd[(2 * hd) * ls:(2 * hd + 1) * ls, :], 0.0)
        o1 = o1 + jnp.where(sel, obd[(2 * hd + 1) * ls:(2 * hd + 2) * ls, :], 0.0)
    o = o0 - _diff_lambda(lam_ref, lambda_init) * o1
    sub = sub_ref[...]
    segs = []
    for hd in range(DIFF_HEADS):
        segs.append(_rms(o[:, hd * DIFF_VDIM:(hd + 1) * DIFF_VDIM], sub) * (1.0 - lambda_init))
    o_ref[...] = jnp.concatenate(segs, axis=1).astype(o_ref.dtype)


def _diff_sample(qd, kd, vd, past_k, past_v, lam_p, subln, prev, geo, lambda_init):
    ls, bs, P = geo.ls, geo.bs, geo.past
    width = DIFF_HEADS * DIFF_VDIM
    row0 = geo.sample0 // ls
    rmap = lambda s: (row0 + s, 0)
    mmap = lambda s: (geo.meta0 // N_META, 0)
    full = lambda a: pl.BlockSpec(a.shape, lambda s: (0,) * a.ndim)
    kern = functools.partial(_diff_sample_kernel, lambda_init=lambda_init, past=P)
    return pl.pallas_call(
        kern,
        grid=(bs,),
        in_specs=[pl.BlockSpec((ls, width), rmap), pl.BlockSpec((ls, width), rmap), pl.BlockSpec((ls, width), rmap),
                  pl.BlockSpec((N_META, width), mmap), pl.BlockSpec((N_META, width), mmap),
                  pl.BlockSpec((1, P, width), lambda s: (s, 0, 0)), pl.BlockSpec((1, P, width), lambda s: (s, 0, 0)),
                  full(lam_p), full(subln),
                  pl.BlockSpec(memory_space=pl.ANY)],
        out_specs=pl.BlockSpec((ls, width), rmap),
        out_shape=jax.ShapeDtypeStruct(prev.shape, prev.dtype),
        input_output_aliases={9: 0},
        compiler_params=_cparams(("parallel",)),
        name="diff_sample",
    )(qd, kd, vd, kd, vd, past_k, past_v, lam_p, subln, prev)


def _mla_sample_kernel(q_ref, kn_ref, km_ref, cp_ref, rp_ref, wuv_ref, prev_ref, o_ref, *, past):
    del prev_ref
    ls = q_ref.shape[0]
    q = q_ref[...]
    qst = jnp.concatenate([q[:, hd * MLA_QW:(hd + 1) * MLA_QW] for hd in range(MLA_HEADS)], axis=0)
    ckv = cp_ref[0].astype(BF16)
    kro = rp_ref[0].astype(BF16)
    kn = kn_ref[...]
    km = km_ref[...]
    s_meta = _nt_dot(qst, km)
    s_past = (_nt_dot(qst[:, 0:MLA_KV_RANK], ckv)
              + _nt_dot(qst[:, MLA_KV_RANK:MLA_KV_RANK + MLA_ROPE], kro))
    s_new = _nt_dot(qst, kn)
    rr, cc = _block_iotas(MLA_HEADS * ls, ls)
    s_new = jnp.where((past + cc) // CHUNK <= (past + rr % ls) // CHUNK, s_new, NEG)
    olat = _three_part_softmax([(s_meta, km[:, 0:MLA_KV_RANK]), (s_past, ckv), (s_new, kn[:, 0:MLA_KV_RANK])])
    outs = []
    for hd in range(MLA_HEADS):
        outs.append(_dot(olat[hd * ls:(hd + 1) * ls, :].astype(BF16), wuv_ref[hd]))
    o_ref[...] = jnp.concatenate(outs, axis=1).astype(o_ref.dtype)


def _mla_sample(qcat, kcat, past_ckv, past_kr, wuv, prev, geo):
    ls, bs, P = geo.ls, geo.bs, geo.past
    row0 = geo.sample0 // ls
    rmap = lambda s: (row0 + s, 0)
    kern = functools.partial(_mla_sample_kernel, past=P)
    return pl.pallas_call(
        kern,
        grid=(bs,),
        in_specs=[pl.BlockSpec((ls, MLA_HEADS * MLA_QW), rmap),
                  pl.BlockSpec((ls, MLA_QW), rmap),
                  pl.BlockSpec((N_META, MLA_QW), lambda s: (geo.meta0 // N_META, 0)),
                  pl.BlockSpec((1, P, MLA_KV_RANK), lambda s: (s, 0, 0)),
                  pl.BlockSpec((1, P, MLA_ROPE), lambda s: (s, 0, 0)),
                  pl.BlockSpec(wuv.shape, lambda s: (0, 0, 0)),
                  pl.BlockSpec(memory_space=pl.ANY)],
        out_specs=pl.BlockSpec((ls, MLA_HEADS * MLA_VDIM), rmap),
        out_shape=jax.ShapeDtypeStruct(prev.shape, prev.dtype),
        input_output_aliases={6: 0},
        compiler_params=_cparams(("parallel",)),
        name="mla_sample",
    )(qcat, kcat, kcat, past_ckv, past_kr, wuv, prev)


def _post_attn_kernel(*refs, n_in):
    a_refs = refs[:n_in]
    w_refs = refs[n_in:2 * n_in]
    x_ref, g_ref, wr_ref, br_ref = refs[2 * n_in:2 * n_in + 4]
    x1_ref, xs_ref, gs_ref, lp_ref, cnt_ref = refs[2 * n_in + 4:]

    y = _dot(a_refs[0][...], w_refs[0][...])
    for a_ref, w_ref in zip(a_refs[1:], w_refs[1:]):
        y = y + _dot(a_ref[...], w_ref[...])
    x1 = x_ref[...] + y
    x1_ref[...] = x1
    h = _rms(x1, g_ref[...])

    logits = jnp.dot(h, wr_ref[...], precision=HIGHEST, preferred_element_type=F32) + br_ref[...]
    lane = lax.broadcasted_iota(jnp.int32, (TM, LANES), 1)
    lanef = lane.astype(F32)
    is_g = jnp.logical_and(lane >= N_EXPERTS, lane < N_EXPERTS + N_GROUPS)
    gl = jnp.where(is_g, logits, -jnp.inf)
    gmax = jnp.max(gl, axis=1, keepdims=True)
    gsel = jnp.min(jnp.where(gl == gmax, lanef - N_EXPERTS, 1e9), axis=1, keepdims=True)
    p_g = 1.0 / jnp.sum(jnp.exp(gl - gmax), axis=1, keepdims=True)
    lane_group = (lane // EXPERTS_PER_GROUP).astype(F32)
    in_group = jnp.logical_and(lane < N_EXPERTS, lane_group == gsel)
    el = jnp.where(in_group, logits, -jnp.inf)
    v1 = jnp.max(el, axis=1, keepdims=True)
    i1 = jnp.min(jnp.where(el == v1, lanef, 1e9), axis=1, keepdims=True)
    el2 = jnp.where(lanef == i1, -jnp.inf, el)
    v2 = jnp.max(el2, axis=1, keepdims=True)
    i2 = jnp.min(jnp.where(el2 == v2, lanef, 1e9), axis=1, keepdims=True)
    e2 = jnp.exp(v2 - v1)
    den = 1.0 + e2
    wa = (1.0 / den) * p_g
    wb = (e2 / den) * p_g
    j1 = i1 - EXPERTS_PER_GROUP * gsel
    j2 = i2 - EXPERTS_PER_GROUP * gsel
    gate = jnp.where(lanef == j1, wa, 0.0) + jnp.where(lanef == j2, wb, 0.0)

    ohg = jnp.where(lanef == gsel, 1.0, 0.0)
    counts = jnp.sum(ohg, axis=0, keepdims=True)
    rows16 = jnp.floor((counts + (ROW_CHUNK - 1)) * (1.0 / ROW_CHUNK)) * ROW_CHUNK
    r128, c128 = _block_iotas(LANES, LANES)
    before = jnp.where(r128 < c128, 1.0, 0.0).astype(BF16)
    off = _dot(jnp.broadcast_to(rows16, (8, LANES)).astype(BF16), before)[0:1, :]
    rt, ct = _block_iotas(TM, TM)
    earlier = jnp.where(ct < rt, 1.0, 0.0).astype(BF16)
    rank = _dot(earlier, ohg.astype(BF16))
    lp = jnp.sum(ohg * (off + rank), axis=1, keepdims=True)
    lp_b = jnp.broadcast_to(lp, (TM, LANES))
    lp_ref[...] = lp_b
    cnt_ref[0] = jnp.broadcast_to(counts, (8, LANES))
    lp_row = jnp.transpose(lp_b)[0:1, :]
    rloc = lax.broadcasted_iota(jnp.int32, (LOCAL_ROWS, TM), 0).astype(F32)
    perm = jnp.where(rloc == lp_row, 1.0, 0.0)
    xs_ref[...] = _dot(perm.astype(BF16), h.astype(BF16)).astype(BF16)
    gs_ref[...] = jnp.dot(perm, gate, precision=HIGHEST, preferred_element_type=F32)


def _post_attn(attn_list, w_list, x, g, wr, br):
    T = x.shape[0]
    nt = T // TM
    row = lambda w: pl.BlockSpec((TM, w), lambda i: (i, 0))
    full = lambda a: pl.BlockSpec(a.shape, lambda i: (0,) * a.ndim)
    kern = functools.partial(_post_attn_kernel, n_in=len(attn_list))
    outs = [jax.ShapeDtypeStruct((T, x.shape[1]), F32),
            jax.ShapeDtypeStruct((nt * LOCAL_ROWS, x.shape[1]), BF16),
            jax.ShapeDtypeStruct((nt * LOCAL_ROWS, LANES), F32),
            jax.ShapeDtypeStruct((T, LANES), F32),
            jax.ShapeDtypeStruct((nt, 8, LANES), F32)]
    out_specs = [row(x.shape[1]),
                 pl.BlockSpec((LOCAL_ROWS, x.shape[1]), lambda i: (i, 0)),
                 pl.BlockSpec((LOCAL_ROWS, LANES), lambda i: (i, 0)),
                 row(LANES),
                 pl.BlockSpec((1, 8, LANES), lambda i: (i, 0, 0))]
    return pl.pallas_call(
        kern,
        grid=(nt,),
        in_specs=([row(a.shape[1]) for a in attn_list] + [full(w) for w in w_list]
                  + [row(x.shape[1]), full(g), full(wr), full(br)]),
        out_specs=out_specs,
        out_shape=outs,
        compiler_params=_cparams(("parallel",)),
        name="post_attn_route",
    )(*attn_list, *w_list, x, g, wr, br)


def _chunk_tables(cnt, n_tiles):
    nt = cnt.shape[0]
    chunks = jnp.ceil(cnt / ROW_CHUNK).astype(jnp.int32)
    cum = jnp.cumsum(chunks, axis=1)
    c = jnp.arange(LOCAL_CHUNKS, dtype=jnp.int32)
    keys = jnp.sum(c[None, :, None] >= cum[:, None, :], axis=2).reshape(-1)
    order = jnp.argsort(keys, stable=True).astype(jnp.int32)
    ng = N_GROUPS + 1
    n_g = jnp.sum(keys[:, None] == jnp.arange(ng)[None, :], axis=0).astype(jnp.int32)
    tiles_g = (n_g + EXPERT_TILE_CHUNKS - 1) // EXPERT_TILE_CHUNKS
    tile_end = jnp.cumsum(tiles_g)
    start_slot = (tile_end - tiles_g) * EXPERT_TILE_CHUNKS
    start_sorted = jnp.cumsum(n_g) - n_g
    gj = keys[order]
    j = jnp.arange(nt * LOCAL_CHUNKS, dtype=jnp.int32)
    slot = start_slot[gj] + (j - start_sorted[gj])
    src = jnp.full((n_tiles * EXPERT_TILE_CHUNKS,), -1, jnp.int32).at[slot].set(order)
    wgrp = jnp.sum(jnp.arange(n_tiles)[:, None] >= tile_end[None, :], axis=1)
    wgrp = jnp.minimum(wgrp, N_GROUPS - 1).astype(jnp.int32)
    return src, wgrp


def _experts_kernel(src_ref, wgrp_ref, xs_hbm, gs_hbm, wg_ref, wu_ref, wd_ref, ys_hbm,
                    xbuf, gbuf, obuf, sem_x, sem_g, sem_o):
    del wgrp_ref
    i = pl.program_id(0)
    base = i * EXPERT_TILE_CHUNKS

    def chunk_rows(cid):
        return pl.ds(pl.multiple_of(cid * ROW_CHUNK, ROW_CHUNK), ROW_CHUNK)

    def in_copies(c):
        sid = src_ref[base + c]
        rid = jnp.where(sid < 0, LOCAL_CHUNKS - 1, sid)
        dst = pl.ds(c * ROW_CHUNK, ROW_CHUNK)
        return (pltpu.make_async_copy(xs_hbm.at[chunk_rows(rid), :], xbuf.at[dst, :], sem_x.at[0]),
                pltpu.make_async_copy(gs_hbm.at[chunk_rows(rid), :], gbuf.at[dst, :], sem_g.at[0]))

    def out_copy(c):
        sid = src_ref[base + c]
        rid = jnp.maximum(sid, 0)
        return sid >= 0, pltpu.make_async_copy(obuf.at[pl.ds(c * ROW_CHUNK, ROW_CHUNK), :],
                                               ys_hbm.at[chunk_rows(rid), :], sem_o.at[0])

    for c in range(EXPERT_TILE_CHUNKS):
        cx, cg = in_copies(c)
        cx.start()
        cg.start()
    for c in range(EXPERT_TILE_CHUNKS):
        cx, cg = in_copies(c)
        cx.wait()
        cg.wait()

    x = xbuf[...]
    a = _dot(x, wg_ref[...])
    b = _dot(x, wu_ref[...])
    gate = gbuf[...]
    parts = []
    for e in range(EXPERTS_PER_GROUP):
        sl = slice(e * EXPERT_FF, (e + 1) * EXPERT_FF)
        ae = a[:, sl]
        parts.append((ae * jax.nn.sigmoid(ae) * b[:, sl] * gate[:, e:e + 1]).astype(BF16))
    hdn = jnp.concatenate(parts, axis=1)
    obuf[...] = _dot(hdn, wd_ref[...])

    for c in range(EXPERT_TILE_CHUNKS):
        valid, co = out_copy(c)

        @pl.when(valid)
        def _():
            co.start()
    for c in range(EXPERT_TILE_CHUNKS):
        valid, co = out_copy(c)

        @pl.when(valid)
        def _():
            co.wait()


def _experts(xs, gs, src, wgrp, wg, wu, wd):
    n_tiles = wgrp.shape[0]
    d = xs.shape[1]
    gw = EXPERTS_PER_GROUP * EXPERT_FF
    grid_spec = pltpu.PrefetchScalarGridSpec(
        num_scalar_prefetch=2,
        grid=(n_tiles,),
        in_specs=[pl.BlockSpec(memory_space=pl.ANY),
                  pl.BlockSpec(memory_space=pl.ANY),
                  pl.BlockSpec((None, d, gw), lambda i, src, wgrp: (wgrp[i], 0, 0)),
                  pl.BlockSpec((None, d, gw), lambda i, src, wgrp: (wgrp[i], 0, 0)),
                  pl.BlockSpec((None, gw, d), lambda i, src, wgrp: (wgrp[i], 0, 0))],
        out_specs=pl.BlockSpec(memory_space=pl.ANY),
        scratch_shapes=[pltpu.VMEM((TM, d), BF16), pltpu.VMEM((TM, LANES), F32), pltpu.VMEM((TM, d), F32),
                        pltpu.SemaphoreType.DMA((1,)), pltpu.SemaphoreType.DMA((1,)), pltpu.SemaphoreType.DMA((1,))],
    )
    return pl.pallas_call(
        _experts_kernel,
        grid_spec=grid_spec,
        out_shape=jax.ShapeDtypeStruct((xs.shape[0], d), F32),
        compiler_params=_cparams(("arbitrary",)),
        name="moe_experts",
    )(src, wgrp, xs, gs, wg, wu, wd)


def _moe_combine_kernel(x1_ref, ys_ref, lp_ref, gfin_ref, o_ref, *, final):
    d = x1_ref.shape[1]
    yl = jnp.concatenate([ys_ref[...], jnp.zeros((LOCAL_ROWS_PAD - LOCAL_ROWS, d), F32)], axis=0)
    lp = lp_ref[...][:, 0:1]
    lanef = lax.broadcasted_iota(jnp.int32, (TM, LOCAL_ROWS_PAD), 1).astype(F32)
    sel = jnp.where(lanef == lp, 1.0, 0.0).astype(BF16)
    hi = yl.astype(BF16)
    r1 = yl - hi.astype(F32)
    mid = r1.astype(BF16)
    lo = (r1 - mid.astype(F32)).astype(BF16)
    moe = (_dot(sel, hi) + _dot(sel, mid)) + _dot(sel, lo)
    x2 = x1_ref[...] + moe
    if final:
        x2 = _rms(x2, gfin_ref[...])
    o_ref[...] = x2


def _moe_combine(x1, ys, lp, gfin, final):
    T, d = x1.shape
    row = lambda w: pl.BlockSpec((TM, w), lambda i: (i, 0))
    kern = functools.partial(_moe_combine_kernel, final=final)
    return pl.pallas_call(
        kern,
        grid=(T // TM,),
        in_specs=[row(d), pl.BlockSpec((LOCAL_ROWS, d), lambda i: (i, 0)), row(LANES),
                  pl.BlockSpec(gfin.shape, lambda i: (0, 0))],
        out_specs=row(d),
        out_shape=jax.ShapeDtypeStruct((T, d), F32),
        compiler_params=_cparams(("parallel",)),
        name="moe_combine",
    )(x1, ys, lp, gfin)


class _Geometry:
    def __init__(self, bp, lp, bs, ls, past):
        self.bp, self.lp, self.bs, self.ls, self.past = bp, lp, bs, ls, past
        self.prompt_rows = bp * lp
        self.sample0 = self.prompt_rows
        self.sample_rows = bs * ls
        self.meta0 = self.prompt_rows + self.sample_rows
        self.total = self.meta0 + TM
        assert lp % TQ == 0 and self.sample_rows % TM == 0 and ls % ROW_CHUNK == 0


def _rotary_tables(geo):
    pos = np.zeros((geo.total,), np.int32)
    pos[:geo.prompt_rows] = np.tile(N_META + np.arange(geo.lp), geo.bp)
    pos[geo.sample0:geo.meta0] = np.tile(N_META + geo.past + np.arange(geo.ls), geo.bs)
    pos[geo.meta0:geo.meta0 + N_META] = np.arange(N_META)
    inv = ROPE_THETA ** (-jnp.arange(0, HEAD_DIM, 2, dtype=F32) / HEAD_DIM)
    ang = jnp.asarray(pos).astype(F32)[:, None] * inv[None, :]
    cos = jnp.concatenate([jnp.cos(ang), jnp.cos(ang)], -1)
    sin = jnp.concatenate([jnp.sin(ang), jnp.sin(ang)], -1)
    zeros = jnp.zeros_like(cos)
    return (jnp.concatenate([cos, cos], 1), jnp.concatenate([sin, sin], 1),
            jnp.concatenate([cos, zeros], 1), jnp.concatenate([sin, zeros], 1))


def _rot_cols(w, group):
    shp = w.shape
    wg = w.reshape(shp[:-1] + (shp[-1] // group, 2, group // 2))
    return jnp.concatenate([-wg[..., 1:2, :], wg[..., 0:1, :]], axis=-2).reshape(shp)


def _moe_layer(attn_list, w_list, x, g_ffn, wr, br, wg, wu, wd, gfin, final):
    nt = x.shape[0] // TM
    n_tiles = -(-(nt * LOCAL_CHUNKS) // EXPERT_TILE_CHUNKS) + N_GROUPS + 1
    x1, xs, gs, lp, cnt = _post_attn(attn_list, w_list, x, g_ffn, wr, br)
    src, wgrp = _chunk_tables(cnt[:, 0, :N_GROUPS], n_tiles)
    ys = _experts(xs, gs, src, wgrp, wg, wu, wd)
    return _moe_combine(x1, ys, lp, gfin, final)


def _router_weights(w_group, b_group, w_router, b_router):
    d = w_group.shape[0]
    wr = jnp.zeros((d, LANES), F32).at[:, :N_EXPERTS].set(w_router).at[:, N_EXPERTS:N_EXPERTS + N_GROUPS].set(w_group)
    br = jnp.zeros((1, LANES), F32).at[0, :N_EXPERTS].set(b_router).at[0, N_EXPERTS:N_EXPERTS + N_GROUPS].set(b_group)
    return wr, br


def _expert_weights(w_gate, w_up, w_down):
    d = w_gate.shape[1]

    def cat(w):
        w = w.reshape(N_GROUPS, EXPERTS_PER_GROUP, d, EXPERT_FF).transpose(0, 2, 1, 3)
        return w.reshape(N_GROUPS, d, EXPERTS_PER_GROUP * EXPERT_FF).astype(BF16)

    return cat(w_gate), cat(w_up), w_down.reshape(N_GROUPS, EXPERTS_PER_GROUP * EXPERT_FF, d).astype(BF16)


def kernel(x_prompt, x_sample, cache_fox_k, cache_fox_v, cache_fox_logf, cache_diff_k, cache_diff_v, cache_mla_ckv, cache_mla_krope, meta_tokens, norm_mix, norm_ffn, norm_final, w_in_even, fox_b_f, diff_lambda, diff_subln, w_out_even, w_in_odd, mla_norm_q, mla_norm_kv, mla_w_uq, mla_w_uk, mla_w_uv, w_out_odd, moe_w_group, moe_b_group, moe_w_router, moe_b_router, moe_w_gate, moe_w_up, moe_w_down):
    bp, lp, d = x_prompt.shape
    bs, ls, _ = x_sample.shape
    past = cache_fox_k.shape[2]
    geo = _Geometry(bp, lp, bs, ls, past)
    T, PR, SR, M0 = geo.total, geo.prompt_rows, geo.sample_rows, geo.meta0
    cos2, sin2, cosp, sinp = _rotary_tables(geo)

    x = jnp.concatenate([x_prompt.reshape(PR, d), x_sample.reshape(SR, d), meta_tokens,
                         jnp.zeros((TM - N_META, d), F32)], axis=0)

    w = w_in_even[0]
    o = 3 * FOX_W
    wq_d = w[:, o + FOX_HEADS:o + FOX_HEADS + 512]
    wk_d = w[:, o + FOX_HEADS + 512:o + FOX_HEADS + 1024]
    wv_d = w[:, o + FOX_HEADS + 1024:o + FOX_HEADS + 1536]
    wf = jnp.zeros((d, LANES), F32).at[:, :FOX_HEADS].set(w[:, o:o + FOX_HEADS])
    w_all = jnp.concatenate([w[:, 0:o], wf, wq_d, _rot_cols(wq_d, HEAD_DIM), wk_d, _rot_cols(wk_d, HEAD_DIM), wv_d],
                            axis=1).astype(BF16)
    b_f = jnp.zeros((1, LANES), F32).at[0, :FOX_HEADS].set(fox_b_f[0])
    (qf, kf, vf, qd, kd, vd, kf32, vf32, lf, kd32, vd32) = _proj_even(
        x, norm_mix[0][None, :], w_all, b_f, cos2, sin2)

    lf8 = lf[:, :FOX_HEADS]
    meta_lf = lf8[M0:M0 + N_META]
    rows_p = jnp.concatenate([jnp.broadcast_to(meta_lf[None], (bp, N_META, FOX_HEADS)),
                              lf8[:PR].reshape(bp, lp, FOX_HEADS)], axis=1)
    rows_p = rows_p.transpose(0, 2, 1).reshape(bp * FOX_HEADS, N_META + lp)
    rows_s = jnp.concatenate([jnp.broadcast_to(meta_lf[None], (bs, N_META, FOX_HEADS)),
                              cache_fox_logf[0], lf8[PR:PR + SR].reshape(bs, ls, FOX_HEADS)], axis=1)
    rows_s = rows_s.transpose(0, 2, 1).reshape(bs * FOX_HEADS, N_META + past + ls)
    c_p = _cumsum_rows(rows_p)
    c_s = _cumsum_rows(rows_s)
    c_meta_row = c_p[:FOX_HEADS, :N_META]
    c_prompt = c_p[:, N_META:N_META + lp].reshape(bp, FOX_HEADS, lp)
    c_past = c_s[:, N_META:N_META + past].reshape(bs, FOX_HEADS, past)
    c_new = c_s[:, N_META + past:N_META + past + ls].reshape(bs, FOX_HEADS, ls)
    c_col8 = jnp.concatenate([c_prompt.transpose(0, 2, 1).reshape(PR, FOX_HEADS),
                              c_new.transpose(0, 2, 1).reshape(SR, FOX_HEADS),
                              c_meta_row.T, jnp.zeros((TM - N_META, FOX_HEADS), F32)], axis=0)
    c_col = c_col8.reshape(T, FOX_HEADS // 2, 2).transpose(1, 0, 2)
    c_row = c_prompt.reshape(bp, FOX_HEADS // 2, 2, lp)
    c_row_meta = c_meta_row.reshape(FOX_HEADS // 2, 2, N_META)

    lambda_init = 0.8 - 0.6 * math.exp(-0.3 * 0)
    lam_p = diff_lambda[0]
    subln = diff_subln[0][None, :]
    fox_o = _fox_prompt(qf, kf, vf, c_col, c_row, c_row_meta, geo)
    fox_o = _fox_sample(qf, kf, vf, cache_fox_k[0].reshape(bs, past, FOX_W), cache_fox_v[0].reshape(bs, past, FOX_W),
                        c_col8, c_meta_row, c_past, c_new, fox_o, geo)
    diff_o = _diff_prompt(qd, kd, vd, lam_p, subln, geo, lambda_init)
    diff_o = _diff_sample(qd, kd, vd, cache_diff_k[0].reshape(bs, past, 512), cache_diff_v[0].reshape(bs, past, 512),
                          lam_p, subln, diff_o, geo, lambda_init)

    wo = w_out_even[0].astype(BF16)
    wr, br = _router_weights(moe_w_group[0], moe_b_group[0], moe_w_router[0], moe_b_router[0])
    wg, wu, wd = _expert_weights(moe_w_gate[0], moe_w_up[0], moe_w_down[0])
    x = _moe_layer([fox_o, diff_o], [wo[:FOX_W], wo[FOX_W:]], x, norm_ffn[0][None, :], wr, br, wg, wu, wd,
                   norm_final[None, :], False)

    w = w_in_odd[0]
    o = MLA_Q_RANK + MLA_KV_RANK
    wkr = w[:, o:o + MLA_ROPE]
    z64 = jnp.zeros((d, MLA_ROPE), F32)
    w_all = jnp.concatenate([w[:, :o], wkr, z64, _rot_cols(wkr, MLA_ROPE), z64], axis=1).astype(BF16)
    uq = mla_w_uq[0]
    uq_rope = uq[:, :, MLA_NOPE:]
    zr = jnp.zeros_like(uq_rope)
    wuq = jnp.concatenate([uq[:, :, :MLA_NOPE].reshape(MLA_Q_RANK, -1),
                           jnp.concatenate([uq_rope, zr], -1).reshape(MLA_Q_RANK, -1),
                           jnp.concatenate([_rot_cols(uq_rope, MLA_ROPE), zr], -1).reshape(MLA_Q_RANK, -1)],
                          axis=1).astype(BF16)
    wuk = mla_w_uk[0].transpose(1, 2, 0).astype(BF16)
    wuv = mla_w_uv[0].transpose(1, 0, 2).astype(BF16)
    qcat, kcat, ckv32, kr32 = _proj_odd(x, norm_mix[1][None, :], w_all, mla_norm_q[0][None, :],
                                        mla_norm_kv[0][None, :], wuq, wuk, cosp, sinp)
    mla_o = _mla_prompt(qcat, kcat, wuv, geo)
    mla_o = _mla_sample(qcat, kcat, cache_mla_ckv[0], cache_mla_krope[0], wuv, mla_o, geo)

    wr, br = _router_weights(moe_w_group[1], moe_b_group[1], moe_w_router[1], moe_b_router[1])
    wg, wu, wd = _expert_weights(moe_w_gate[1], moe_w_up[1], moe_w_down[1])
    y = _moe_layer([mla_o], [w_out_odd[0].astype(BF16)], x, norm_ffn[1][None, :], wr, br, wg, wu, wd,
                   norm_final[None, :], True)

    def prompt_rows(a):
        w_ = a.shape[1]
        meta = jnp.broadcast_to(a[M0:M0 + N_META][None], (bp, N_META, w_))
        return jnp.concatenate([meta, a[:PR].reshape(bp, lp, w_)], axis=1)[None]

    def sample_rows(a):
        return a[PR:PR + SR].reshape(1, bs, ls, a.shape[1])

    lf8 = lf[:, :FOX_HEADS]
    kr = kr32[:, :MLA_ROPE]
    y_prompt = y[:PR].reshape(bp, lp, d)
    y_sample = y[PR:PR + SR].reshape(bs, ls, d)
    L = N_META + lp
    return (y_prompt, y_sample,
            prompt_rows(kf32).reshape(1, bp, L, FOX_HEADS, HEAD_DIM),
            prompt_rows(vf32).reshape(1, bp, L, FOX_HEADS, HEAD_DIM),
            prompt_rows(lf8),
            prompt_rows(kd32).reshape(1, bp, L, DIFF_HEADS, 2, HEAD_DIM),
            prompt_rows(vd32).reshape(1, bp, L, DIFF_HEADS, DIFF_VDIM),
            prompt_rows(ckv32), prompt_rows(kr),
            sample_rows(kf32).reshape(1, bs, ls, FOX_HEADS, HEAD_DIM),
            sample_rows(vf32).reshape(1, bs, ls, FOX_HEADS, HEAD_DIM),
            sample_rows(lf8),
            sample_rows(kd32).reshape(1, bs, ls, DIFF_HEADS, 2, HEAD_DIM),
            sample_rows(vd32).reshape(1, bs, ls, DIFF_HEADS, DIFF_VDIM),
            sample_rows(ckv32), sample_rows(kr))
```

```python
import functools
import math

import numpy as np
import jax
import jax.numpy as jnp
from jax import lax
from jax.experimental import pallas as pl
from jax.experimental.pallas import tpu as pltpu

F32 = jnp.float32
BF16 = jnp.bfloat16
HIGHEST = lax.Precision.HIGHEST

CHUNK = 64
N_META = 16
HEAD_DIM = 64
FOX_HEADS = 8
DIFF_HEADS = 4
DIFF_VDIM = 128
ROPE_THETA = 10000.0
MLA_HEADS = 8
MLA_Q_RANK = 384
MLA_KV_RANK = 256
MLA_NOPE = 128
MLA_ROPE = 64
MLA_VDIM = 128
N_GROUPS = 4
EXPERTS_PER_GROUP = 8
N_EXPERTS = N_GROUPS * EXPERTS_PER_GROUP
EXPERT_FF = 256
RMS_EPS = 1e-6
FOX_W = FOX_HEADS * HEAD_DIM
MLA_QW = MLA_KV_RANK + 128

LANES = 128
TM = 256
TQ = 256
ROW_CHUNK = 16
LOCAL_CHUNKS = 21
LOCAL_ROWS = LOCAL_CHUNKS * ROW_CHUNK
LOCAL_ROWS_PAD = 384
EXPERT_TILE_CHUNKS = TM // ROW_CHUNK
NEG = -1e30
VMEM_LIMIT = 56 * 1024 * 1024


def _cparams(sem):
    return pltpu.CompilerParams(dimension_semantics=sem, vmem_limit_bytes=VMEM_LIMIT)


def _rms(x, g):
    ms = jnp.mean(x * x, axis=-1, keepdims=True)
    return (x * lax.rsqrt(ms + RMS_EPS)) * g


def _nt_dot(a, b):
    return lax.dot_general(a, b, (((1,), (1,)), ((), ())), preferred_element_type=F32)


def _dot(a, b):
    return jnp.dot(a, b, preferred_element_type=F32)


def _proj_even_kernel(x_ref, g_ref, w_ref, bf_ref, cs_ref, sn_ref,
                      qf_ref, kf_ref, vf_ref, qd_ref, kd_ref, vd_ref,
                      kf32_ref, vf32_ref, lf_ref, kd32_ref, vd32_ref):
    h = _rms(x_ref[...], g_ref[...]).astype(BF16)

    def mm(i, width=FOX_W):
        return _dot(h, w_ref[:, i:i + width])

    scale = HEAD_DIM ** -0.5
    qf = mm(0)
    kf = mm(512)
    vf = mm(1024)
    fl = mm(1536, LANES)
    qd = mm(1664)
    qdr = mm(2176)
    kd = mm(2688)
    kdr = mm(3200)
    vd = mm(3712)
    cos = jnp.concatenate([cs_ref[...]] * 4, axis=1)
    sin = jnp.concatenate([sn_ref[...]] * 4, axis=1)
    qd = qd * cos + qdr * sin
    kd = kd * cos + kdr * sin
    z = fl + bf_ref[...]
    lf_ref[...] = jnp.minimum(z, 0.0) - jnp.log1p(jnp.exp(-jnp.abs(z)))
    qf_ref[...] = (qf * scale).astype(BF16)
    kf_ref[...] = kf.astype(BF16)
    vf_ref[...] = vf.astype(BF16)
    qd_ref[...] = (qd * scale).astype(BF16)
    kd_ref[...] = kd.astype(BF16)
    vd_ref[...] = vd.astype(BF16)
    kf32_ref[...] = kf
    vf32_ref[...] = vf
    kd32_ref[...] = kd
    vd32_ref[...] = vd


def _proj_even(x, g, w_all, b_f, cos2, sin2):
    T = x.shape[0]
    row = lambda w: pl.BlockSpec((TM, w), lambda i: (i, 0))
    full = lambda a: pl.BlockSpec(a.shape, lambda i: (0,) * a.ndim)
    outs = ([jax.ShapeDtypeStruct((T, FOX_W), BF16)] * 6
            + [jax.ShapeDtypeStruct((T, FOX_W), F32)] * 2
            + [jax.ShapeDtypeStruct((T, LANES), F32)]
            + [jax.ShapeDtypeStruct((T, FOX_W), F32)] * 2)
    out_specs = [row(FOX_W)] * 8 + [row(LANES)] + [row(FOX_W)] * 2
    return pl.pallas_call(
        _proj_even_kernel,
        grid=(T // TM,),
        in_specs=[row(x.shape[1]), full(g), full(w_all), full(b_f), row(LANES), row(LANES)],
        out_specs=out_specs,
        out_shape=outs,
        compiler_params=_cparams(("parallel",)),
        name="proj_even",
    )(x, g, w_all, b_f, cos2, sin2)


def _proj_odd_kernel(x_ref, g_ref, w_ref, gq_ref, gkv_ref, wuq_ref, wuk_ref, cs_ref, sn_ref,
                     q_ref, kcat_ref, ckv32_ref, kr32_ref):
    h = _rms(x_ref[...], g_ref[...]).astype(BF16)
    scale = (MLA_NOPE + MLA_ROPE) ** -0.5
    cq = _dot(h, w_ref[:, 0:MLA_Q_RANK])
    ckv = _dot(h, w_ref[:, MLA_Q_RANK:MLA_Q_RANK + MLA_KV_RANK])
    kr = _dot(h, w_ref[:, 640:768])
    krr = _dot(h, w_ref[:, 768:896])
    cos = cs_ref[...]
    sin = sn_ref[...]
    kr = kr * cos + krr * sin
    ckv = _rms(ckv, gkv_ref[...])
    ckv32_ref[...] = ckv
    kr32_ref[...] = kr
    kcat_ref[...] = jnp.concatenate([ckv, kr], axis=1).astype(BF16)
    cqn = _rms(cq, gq_ref[...]).astype(BF16)
    hw = MLA_HEADS * LANES
    q_nope = _dot(cqn, wuq_ref[:, 0:hw])
    q_rope = _dot(cqn, wuq_ref[:, hw:2 * hw])
    q_rope_r = _dot(cqn, wuq_ref[:, 2 * hw:3 * hw])
    cos8 = jnp.concatenate([cos] * MLA_HEADS, axis=1)
    sin8 = jnp.concatenate([sin] * MLA_HEADS, axis=1)
    q_rope = (q_rope * cos8 + q_rope_r * sin8) * scale
    pieces = []
    for hd in range(MLA_HEADS):
        qn = q_nope[:, hd * LANES:(hd + 1) * LANES].astype(BF16)
        pieces.append((_dot(qn, wuk_ref[hd]) * scale).astype(BF16))
        pieces.append(q_rope[:, hd * LANES:(hd + 1) * LANES].astype(BF16))
    q_ref[...] = jnp.concatenate(pieces, axis=1)


def _proj_odd(x, g, w_all, gq, gkv, wuq, wuk, cosp, sinp):
    T = x.shape[0]
    row = lambda w: pl.BlockSpec((TM, w), lambda i: (i, 0))
    full = lambda a: pl.BlockSpec(a.shape, lambda i: (0,) * a.ndim)
    outs = [jax.ShapeDtypeStruct((T, MLA_HEADS * MLA_QW), BF16),
            jax.ShapeDtypeStruct((T, MLA_QW), BF16),
            jax.ShapeDtypeStruct((T, MLA_KV_RANK), F32),
            jax.ShapeDtypeStruct((T, LANES), F32)]
    return pl.pallas_call(
        _proj_odd_kernel,
        grid=(T // TM,),
        in_specs=[row(x.shape[1]), full(g), full(w_all), full(gq), full(gkv), full(wuq), full(wuk),
                  row(LANES), row(LANES)],
        out_specs=[row(MLA_HEADS * MLA_QW), row(MLA_QW), row(MLA_KV_RANK), row(LANES)],
        out_shape=outs,
        compiler_params=_cparams(("parallel",)),
        name="proj_odd",
    )(x, g, w_all, gq, gkv, wuq, wuk, cosp, sinp)


def _cumsum_kernel(x_ref, o_ref):
    rows, length = x_ref.shape
    r = lax.broadcasted_iota(jnp.int32, (LANES, LANES), 0)
    c = lax.broadcasted_iota(jnp.int32, (LANES, LANES), 1)
    tri = jnp.where(r <= c, 1.0, 0.0).astype(BF16)

    def body(i, carry):
        off = pl.multiple_of(i * LANES, LANES)
        blk = x_ref[:, pl.ds(off, LANES)]
        hi = blk.astype(BF16)
        r1 = blk - hi.astype(F32)
        mid = r1.astype(BF16)
        lo = (r1 - mid.astype(F32)).astype(BF16)
        cs = (_dot(hi, tri) + _dot(mid, tri)) + _dot(lo, tri) + carry
        o_ref[:, pl.ds(off, LANES)] = cs
        return cs[:, LANES - 1:LANES]

    lax.fori_loop(0, length // LANES, body, jnp.zeros((rows, 1), F32))


def _cumsum_rows(x):
    pad = (-x.shape[1]) % LANES
    xp = jnp.pad(x, ((0, 0), (0, pad)))
    return pl.pallas_call(
        _cumsum_kernel,
        out_shape=jax.ShapeDtypeStruct(xp.shape, F32),
        compiler_params=pltpu.CompilerParams(vmem_limit_bytes=VMEM_LIMIT),
        name="logf_cumsum",
    )(xp)


def _softmax_step(s, m, l, v, rowc=None):
    smax = jnp.max(s, axis=1, keepdims=True)
    if rowc is not None:
        smax = smax + rowc
    m_new = jnp.maximum(m, smax)
    shift = m_new if rowc is None else m_new - rowc
    alpha = jnp.exp(m - m_new)
    p = jnp.exp(s - shift)
    l_new = alpha * l + jnp.sum(p, axis=1, keepdims=True)
    pv = _dot(p.astype(BF16), v)
    return m_new, l_new, alpha, pv


def _block_iotas(rows, cols):
    r = lax.broadcasted_iota(jnp.int32, (rows, cols), 0)
    c = lax.broadcasted_iota(jnp.int32, (rows, cols), 1)
    return r, c


def _frame_blocks(qq, nqb, n_prompt_blocks):
    tail = qq >= n_prompt_blocks
    return jnp.where(tail, 0, qq % nqb), jnp.where(tail, 0, qq % nqb + 1)


def _prompt_batch(qq, nqb, n_batch):
    return jnp.minimum(qq // nqb, n_batch - 1)


def _fox_prompt_kernel(q_ref, k_ref, v_ref, km_ref, vm_ref, cq_ref, ck_ref, ckm_ref, o_ref, *, nqb, n_prompt_blocks):
    qq = pl.program_id(1)
    is_meta = qq == pl.num_programs(1) - 1
    n_full, n_all = _frame_blocks(qq, nqb, n_prompt_blocks)
    q = q_ref[...]
    lane = lax.broadcasted_iota(jnp.int32, (TQ, LANES), 1)
    lo = lane < HEAD_DIM
    zero = jnp.zeros_like(q)
    qh = (jnp.where(lo, q, zero), jnp.where(lo, zero, q))
    cq = cq_ref[0]
    cqh = (cq[:, 0:1], cq[:, 1:2])

    def merge(carry, upd):
        (m0, l0, m1, l1, acc) = carry
        (m0n, l0n, a0, pv0), (m1n, l1n, a1, pv1) = upd
        acc = acc * jnp.where(lo, a0, a1) + jnp.where(lo, pv0, pv1)
        return (m0n, l0n, m1n, l1n, acc)

    init = (jnp.full((TQ, 1), NEG, F32), jnp.zeros((TQ, 1), F32),
            jnp.full((TQ, 1), NEG, F32), jnp.zeros((TQ, 1), F32),
            jnp.zeros((TQ, LANES), F32))

    km = km_ref[...]
    vm = vm_ref[...]
    r16, c16 = _block_iotas(TQ, N_META)
    hide = jnp.logical_and(is_meta, c16 > r16)
    upd = []
    for hh in range(2):
        s = _nt_dot(qh[hh], km) - ckm_ref[0, hh:hh + 1, :]
        s = jnp.where(hide, NEG, s)
        upd.append(_softmax_step(s, init[2 * hh], init[2 * hh + 1], vm, cqh[hh]))
    carry = merge(init, upd)

    rr, cc = _block_iotas(TQ, TQ)
    causal = cc <= rr

    def step(kb, carry, masked):
        off = pl.multiple_of(kb * TQ, TQ)
        kblk = k_ref[pl.ds(off, TQ), :]
        vblk = v_ref[pl.ds(off, TQ), :]
        upd = []
        for hh in range(2):
            s = _nt_dot(qh[hh], kblk) - ck_ref[0, 0, hh:hh + 1, pl.ds(off, TQ)]
            if masked:
                s = jnp.where(causal, s, NEG)
            upd.append(_softmax_step(s, carry[2 * hh], carry[2 * hh + 1], vblk, cqh[hh]))
        return merge(carry, upd)

    carry = lax.fori_loop(0, n_full, lambda kb, c: step(kb, c, False), carry)
    carry = lax.fori_loop(n_full, n_all, lambda kb, c: step(kb, c, True), carry)
    (m0, l0, m1, l1, acc) = carry
    o_ref[...] = (acc / jnp.where(lo, l0, l1)).astype(o_ref.dtype)


def _fox_prompt(qf, kf, vf, c_col, c_row, c_row_meta, geo):
    T = qf.shape[0]
    nqb = geo.lp // TQ
    npb = geo.bp * nqb
    qmap = lambda g, qq: (qq, g)
    kvmap = lambda g, qq: (_prompt_batch(qq, nqb, geo.bp), g)
    mmap = lambda g, qq: (geo.meta0 // N_META, g)
    kern = functools.partial(_fox_prompt_kernel, nqb=nqb, n_prompt_blocks=npb)
    return pl.pallas_call(
        kern,
        grid=(FOX_HEADS // 2, T // TQ),
        in_specs=[pl.BlockSpec((TQ, LANES), qmap),
                  pl.BlockSpec((geo.lp, LANES), kvmap),
                  pl.BlockSpec((geo.lp, LANES), kvmap),
                  pl.BlockSpec((N_META, LANES), mmap),
                  pl.BlockSpec((N_META, LANES), mmap),
                  pl.BlockSpec((1, TQ, 2), lambda g, qq: (g, qq, 0)),
                  pl.BlockSpec((1, 1, 2, geo.lp), lambda g, qq: (_prompt_batch(qq, nqb, geo.bp), g, 0, 0)),
                  pl.BlockSpec((1, 2, N_META), lambda g, qq: (g, 0, 0))],
        out_specs=pl.BlockSpec((TQ, LANES), qmap),
        out_shape=jax.ShapeDtypeStruct((T, FOX_W), BF16),
        compiler_params=_cparams(("parallel", "arbitrary")),
        name="fox_prompt",
    )(qf, kf, vf, kf, vf, c_col, c_row, c_row_meta)


def _diff_lambda(lam_ref, lambda_init):
    lp = lam_ref[...]
    a = jnp.sum(lp[0:1, :] * lp[1:2, :], axis=1, keepdims=True)
    b = jnp.sum(lp[2:3, :] * lp[3:4, :], axis=1, keepdims=True)
    return jnp.exp(a) - jnp.exp(b) + lambda_init


def _diff_prompt_kernel(q_ref, k_ref, v_ref, km_ref, vm_ref, lam_ref, sub_ref, o_ref, *,
                        nqb, n_prompt_blocks, lambda_init):
    qq = pl.program_id(1)
    n_full, n_all = _frame_blocks(qq, nqb, n_prompt_blocks)
    q = q_ref[...]
    lane = lax.broadcasted_iota(jnp.int32, (TQ, LANES), 1)
    lo = lane < HEAD_DIM
    zero = jnp.zeros_like(q)
    qh = (jnp.where(lo, q, zero), jnp.where(lo, zero, q))

    init = (jnp.full((TQ, 1), NEG, F32), jnp.zeros((TQ, 1), F32), jnp.zeros((TQ, LANES), F32)) * 2

    def update(carry, kblk, vblk, mask):
        out = []
        for hh in range(2):
            m, l, acc = carry[3 * hh:3 * hh + 3]
            s = _nt_dot(qh[hh], kblk)
            if mask is not None:
                s = jnp.where(mask, s, NEG)
            m, l, alpha, pv = _softmax_step(s, m, l, vblk)
            out += [m, l, acc * alpha + pv]
        return tuple(out)

    carry = update(init, km_ref[...], vm_ref[...], None)

    rr, cc = _block_iotas(TQ, TQ)
    chunk_ok = (cc // CHUNK) <= (rr // CHUNK)

    def step(kb, carry, masked):
        off = pl.multiple_of(kb * TQ, TQ)
        return update(carry, k_ref[pl.ds(off, TQ), :], v_ref[pl.ds(off, TQ), :], chunk_ok if masked else None)

    carry = lax.fori_loop(0, n_full, lambda kb, c: step(kb, c, False), carry)
    carry = lax.fori_loop(n_full, n_all, lambda kb, c: step(kb, c, True), carry)
    (m0, l0, acc0, m1, l1, acc1) = carry
    lam = _diff_lambda(lam_ref, lambda_init)
    o = acc0 / l0 - lam * (acc1 / l1)
    o = _rms(o, sub_ref[...]) * (1.0 - lambda_init)
    o_ref[...] = o.astype(o_ref.dtype)


def _diff_prompt(qd, kd, vd, lam_p, subln, geo, lambda_init):
    T = qd.shape[0]
    nqb = geo.lp // TQ
    npb = geo.bp * nqb
    qmap = lambda g, qq: (qq, g)
    kvmap = lambda g, qq: (_prompt_batch(qq, nqb, geo.bp), g)
    mmap = lambda g, qq: (geo.meta0 // N_META, g)
    full = lambda a: pl.BlockSpec(a.shape, lambda g, qq: (0,) * a.ndim)
    kern = functools.partial(_diff_prompt_kernel, nqb=nqb, n_prompt_blocks=npb, lambda_init=lambda_init)
    return pl.pallas_call(
        kern,
        grid=(DIFF_HEADS, T // TQ),
        in_specs=[pl.BlockSpec((TQ, LANES), qmap),
                  pl.BlockSpec((geo.lp, LANES), kvmap),
                  pl.BlockSpec((geo.lp, LANES), kvmap),
                  pl.BlockSpec((N_META, LANES), mmap),
                  pl.BlockSpec((N_META, LANES), mmap),
                  full(lam_p), full(subln)],
        out_specs=pl.BlockSpec((TQ, LANES), qmap),
        out_shape=jax.ShapeDtypeStruct((T, DIFF_HEADS * DIFF_VDIM), BF16),
        compiler_params=_cparams(("parallel", "arbitrary")),
        name="diff_prompt",
    )(qd, kd, vd, kd, vd, lam_p, subln)


def _stacked_flash_step(q, kblk, vblk, m_ref, l_ref, acc_ref, mask=None, bias=None, rowc=None):
    s = _nt_dot(q, kblk)
    if bias is not None:
        s = s - bias
    if mask is not None:
        s = jnp.where(mask, s, NEG)
    smax = jnp.max(s, axis=1, keepdims=True)
    if rowc is not None:
        smax = smax + rowc
    m_old = m_ref[...]
    m_new = jnp.maximum(m_old, smax)
    shift = m_new if rowc is None else m_new - rowc
    alpha = jnp.exp(m_old - m_new)
    p = jnp.exp(s - shift)
    l_ref[...] = alpha * l_ref[...] + jnp.sum(p, axis=1, keepdims=True)
    m_ref[...] = m_new
    acc_ref[...] = acc_ref[...] * alpha + _dot(p.astype(BF16), vblk)


def _mla_prompt_kernel(q_ref, k_ref, km_ref, wuv_ref, o_ref, qst_ref, m_ref, l_ref, acc_ref, *, nqb, n_prompt_blocks):
    qq = pl.program_id(0)
    n_full, n_all = _frame_blocks(qq, nqb, n_prompt_blocks)
    rows = MLA_HEADS * TQ
    for hd in range(MLA_HEADS):
        qst_ref[hd * TQ:(hd + 1) * TQ, :] = q_ref[:, hd * MLA_QW:(hd + 1) * MLA_QW]
    acc_ref[...] = jnp.zeros(acc_ref.shape, F32)
    rr, cc = _block_iotas(TQ, TQ)
    chunk_ok = (cc // CHUNK) <= (rr // CHUNK)

    def update(carry, kblk, mask):
        s_all = _nt_dot(qst_ref[...], kblk)
        out, ps, alphas = [], [], []
        for hd in range(MLA_HEADS):
            m, l = carry[2 * hd:2 * hd + 2]
            s = s_all[hd * TQ:(hd + 1) * TQ, :]
            if mask is not None:
                s = jnp.where(mask, s, NEG)
            m_new = jnp.maximum(m, jnp.max(s, axis=1, keepdims=True))
            alpha = jnp.exp(m - m_new)
            p = jnp.exp(s - m_new)
            out += [m_new, alpha * l + jnp.sum(p, axis=1, keepdims=True)]
            ps.append(p.astype(BF16))
            alphas.append(alpha)
        pv_all = _dot(jnp.concatenate(ps, axis=0), kblk[:, 0:MLA_KV_RANK])
        for hd in range(MLA_HEADS):
            sl = slice(hd * TQ, (hd + 1) * TQ)
            acc_ref[sl, :] = acc_ref[sl, :] * alphas[hd] + pv_all[sl, :]
        return tuple(out)

    def step(kb, carry, masked):
        off = pl.multiple_of(kb * TQ, TQ)
        return update(carry, k_ref[pl.ds(off, TQ), :], chunk_ok if masked else None)

    carry = (jnp.full((TQ, 1), NEG, F32), jnp.zeros((TQ, 1), F32)) * MLA_HEADS
    carry = update(carry, km_ref[...], None)
    carry = lax.fori_loop(0, n_full, lambda kb, c: step(kb, c, False), carry)
    carry = lax.fori_loop(n_full, n_all, lambda kb, c: step(kb, c, True), carry)
    for hd in range(MLA_HEADS):
        sl = slice(hd * TQ, (hd + 1) * TQ)
        o = _dot((acc_ref[sl, :] / carry[2 * hd + 1]).astype(BF16), wuv_ref[hd])
        o_ref[:, hd * MLA_VDIM:(hd + 1) * MLA_VDIM] = o.astype(o_ref.dtype)


def _transpose_bf16(x):
    return jnp.transpose(x.astype(F32)).astype(BF16)


N_PAIRS = 4
BIAS_LANES = 3


def _pair_attn_t_kernel(*refs, nqb, n_prompt_blocks, fox, lambda_init):
    if fox:
        q_ref, k_ref, v_ref, km_ref, vm_ref, c3_ref, c3m_ref, cq_ref, o_ref, qbd_ref, vt_ref, acc_ref = refs
    else:
        q_ref, k_ref, v_ref, km_ref, vm_ref, lam_ref, sub_ref, o_ref, qbd_ref, vt_ref, acc_ref = refs
    qq = pl.program_id(0)
    is_meta = qq == pl.num_programs(0) - 1
    n_full, n_all = _frame_blocks(qq, nqb, n_prompt_blocks)
    cols = 2 * N_PAIRS * TQ
    vrows = HEAD_DIM if fox else DIFF_VDIM

    @pl.when(jnp.logical_and(qq < n_prompt_blocks, qq % nqb == 0))
    def _():
        @pl.loop(0, nqb)
        def _(kb):
            off = pl.multiple_of(kb * TQ, TQ)
            vt_ref[:, pl.ds(off, TQ)] = _transpose_bf16(v_ref[pl.ds(off, TQ), :])

    r2, c2 = _block_iotas(LANES, 2 * TQ)
    first = (r2 < HEAD_DIM) == (c2 < TQ)
    for g in range(N_PAIRS):
        qt = jnp.transpose(q_ref[:, g * LANES:(g + 1) * LANES].astype(F32))
        qt2 = jnp.concatenate([qt, qt], axis=1)
        qbd_ref[g, 0:LANES, :] = jnp.where(first, qt2, 0.0).astype(BF16)
        if fox:
            member = jnp.where(c2 < TQ, 0, 1)
            hit = jnp.logical_and(r2 >= BIAS_LANES * member, r2 < BIAS_LANES * (member + 1))
            qbd_ref[g, LANES:2 * LANES, :] = jnp.where(hit, -1.0, 0.0).astype(BF16)
    acc_ref[...] = jnp.zeros(acc_ref.shape, F32)
    rowc = cq_ref[0] if fox else None

    def update(carry, kblk, c3blk, vt, mask):
        m, l = carry
        parts = []
        for g in range(N_PAIRS):
            kg = kblk[:, g * LANES:(g + 1) * LANES]
            if fox:
                kg = jnp.concatenate([kg, c3blk[:, g * LANES:(g + 1) * LANES]], axis=1)
                parts.append(_dot(kg, qbd_ref[g]))
            else:
                parts.append(_dot(kg, qbd_ref[g, 0:LANES, :]))
        s = jnp.concatenate(parts, axis=1)
        if mask is not None:
            s = jnp.where(mask, s, NEG)
        smax = jnp.max(s, axis=0, keepdims=True)
        if fox:
            smax = smax + rowc
        m_new = jnp.maximum(m, smax)
        shift = m_new - rowc if fox else m_new
        alpha = jnp.exp(m - m_new)
        p = jnp.exp(s - shift)
        l = alpha * l + jnp.sum(p, axis=0, keepdims=True)
        p = p.astype(BF16)
        pvs = []
        for j in range(2 * N_PAIRS):
            vsl = slice(j * vrows, (j + 1) * vrows) if fox else slice((j // 2) * vrows, (j // 2 + 1) * vrows)
            pvs.append(_dot(vt[vsl, :], p[:, j * TQ:(j + 1) * TQ]))
        acc_ref[...] = acc_ref[...] * alpha + jnp.concatenate(pvs, axis=1)
        return m_new, l

    def step(kb, carry, masked):
        off = pl.multiple_of(kb * TQ, TQ)
        mask = None
        if masked:
            kk, qi = _block_iotas(TQ, cols)
            qi = qi % TQ
            mask = (kk <= qi) if fox else ((kk // CHUNK) <= (qi // CHUNK))
        c3blk = c3_ref[pl.ds(off, TQ), :] if fox else None
        return update(carry, k_ref[pl.ds(off, TQ), :], c3blk, vt_ref[:, pl.ds(off, TQ)], mask)

    kk, qi = _block_iotas(LANES, cols)
    mask = kk < N_META
    if fox:
        mask = jnp.logical_and(mask, jnp.logical_or(jnp.logical_not(is_meta), kk <= qi % TQ))
    carry = (jnp.full((1, cols), NEG, F32), jnp.zeros((1, cols), F32))
    carry = update(carry, km_ref[...], c3m_ref[...] if fox else None, _transpose_bf16(vm_ref[...]), mask)
    carry = lax.fori_loop(0, n_full, lambda kb, c: step(kb, c, False), carry)
    carry = lax.fori_loop(n_full, n_all, lambda kb, c: step(kb, c, True), carry)
    o = acc_ref[...] * (1.0 / carry[1])
    if fox:
        for g in range(N_PAIRS):
            pair = jnp.concatenate([o[:, (2 * g) * TQ:(2 * g + 1) * TQ], o[:, (2 * g + 1) * TQ:(2 * g + 2) * TQ]], axis=0)
            o_ref[:, g * LANES:(g + 1) * LANES] = jnp.transpose(pair).astype(o_ref.dtype)
    else:
        lam = _diff_lambda(lam_ref, lambda_init)
        for g in range(N_PAIRS):
            d = o[:, (2 * g) * TQ:(2 * g + 1) * TQ] - lam * o[:, (2 * g + 1) * TQ:(2 * g + 2) * TQ]
            ms = jnp.mean(d * d, axis=0, keepdims=True)
            d = (d * lax.rsqrt(ms + RMS_EPS)) * sub_ref[...] * (1.0 - lambda_init)
            o_ref[:, g * LANES:(g + 1) * LANES] = jnp.transpose(d).astype(o_ref.dtype)


def _pair_attn_t(q, k, v, extra, geo, fox, lambda_init=0.0):
    T = q.shape[0]
    nqb = geo.lp // TQ
    npb = geo.bp * nqb
    width = q.shape[1]
    once = pl.Buffered(1)
    bmap = lambda qq: (_prompt_batch(qq, nqb, geo.bp), 0)
    mmap = lambda qq: (geo.meta0 // LANES, 0)
    in_specs = [pl.BlockSpec((TQ, width), lambda qq: (qq, 0)),
                pl.BlockSpec((geo.lp, width), bmap, pipeline_mode=once),
                pl.BlockSpec((geo.lp, width), bmap, pipeline_mode=once),
                pl.BlockSpec((LANES, width), mmap),
                pl.BlockSpec((LANES, width), mmap)]
    if fox:
        c3, cq_row = extra
        in_specs += [pl.BlockSpec((geo.lp, width), bmap, pipeline_mode=once),
                     pl.BlockSpec((LANES, width), mmap),
                     pl.BlockSpec((1, 1, 2 * N_PAIRS * TQ), lambda qq: (qq, 0, 0))]
        args = (q, k, v, k, v, c3, c3, cq_row)
        vrows = HEAD_DIM
    else:
        lam_p, sub_col = extra
        in_specs += [pl.BlockSpec(lam_p.shape, lambda qq: (0, 0)), pl.BlockSpec(sub_col.shape, lambda qq: (0, 0))]
        args = (q, k, v, k, v, lam_p, sub_col)
        vrows = DIFF_VDIM
    kern = functools.partial(_pair_attn_t_kernel, nqb=nqb, n_prompt_blocks=npb, fox=fox, lambda_init=lambda_init)
    return pl.pallas_call(
        kern,
        grid=(T // TQ,),
        in_specs=in_specs,
        out_specs=pl.BlockSpec((TQ, width), lambda qq: (qq, 0)),
        out_shape=jax.ShapeDtypeStruct((T, width), BF16),
        scratch_shapes=[pltpu.VMEM((N_PAIRS, 2 * LANES, 2 * TQ), BF16),
                        pltpu.VMEM((width, geo.lp), BF16),
                        pltpu.VMEM((vrows, 2 * N_PAIRS * TQ), F32)],
        compiler_params=_cparams(("arbitrary",)),
        name="fox_prompt" if fox else "diff_prompt",
    )(*args)


def _mla_prompt_t_kernel(q_ref, k_ref, km_ref, wuvt_ref, o_ref, qt_ref, vt_ref, acc_ref, *, nqb, n_prompt_blocks):
    qq = pl.program_id(0)
    n_full, n_all = _frame_blocks(qq, nqb, n_prompt_blocks)

    @pl.when(jnp.logical_and(qq < n_prompt_blocks, qq % nqb == 0))
    def _():
        @pl.loop(0, nqb)
        def _(kb):
            off = pl.multiple_of(kb * TQ, TQ)
            vt_ref[:, pl.ds(off, TQ)] = _transpose_bf16(k_ref[pl.ds(off, TQ), 0:MLA_KV_RANK])

    cols = MLA_HEADS * TQ
    for hd in range(MLA_HEADS):
        qt_ref[:, hd * TQ:(hd + 1) * TQ] = _transpose_bf16(q_ref[:, hd * MLA_QW:(hd + 1) * MLA_QW])
    acc_ref[...] = jnp.zeros(acc_ref.shape, F32)

    def update(carry, kblk, vt, mask):
        m, l = carry
        s = _dot(kblk, qt_ref[...])
        if mask is not None:
            s = jnp.where(mask, s, NEG)
        m_new = jnp.maximum(m, jnp.max(s, axis=0, keepdims=True))
        alpha = jnp.exp(m - m_new)
        p = jnp.exp(s - m_new)
        l = alpha * l + jnp.sum(p, axis=0, keepdims=True)
        acc_ref[...] = acc_ref[...] * alpha + _dot(vt, p.astype(BF16))
        return m_new, l

    def step(kb, carry, masked):
        off = pl.multiple_of(kb * TQ, TQ)
        mask = None
        if masked:
            kk, qi = _block_iotas(TQ, cols)
            mask = (kk // CHUNK) <= ((qi % TQ) // CHUNK)
        return update(carry, k_ref[pl.ds(off, TQ), :], vt_ref[:, pl.ds(off, TQ)], mask)

    km = km_ref[...]
    kk, _ = _block_iotas(LANES, cols)
    carry = (jnp.full((1, cols), NEG, F32), jnp.zeros((1, cols), F32))
    carry = update(carry, km, _transpose_bf16(km[:, 0:MLA_KV_RANK]), kk < N_META)
    carry = lax.fori_loop(0, n_full, lambda kb, c: step(kb, c, False), carry)
    carry = lax.fori_loop(n_full, n_all, lambda kb, c: step(kb, c, True), carry)
    inv_l = 1.0 / carry[1]
    for hd in range(MLA_HEADS):
        sl = slice(hd * TQ, (hd + 1) * TQ)
        olat = (acc_ref[:, sl] * inv_l[:, sl]).astype(BF16)
        o = jnp.transpose(_dot(wuvt_ref[hd], olat))
        o_ref[:, hd * MLA_VDIM:(hd + 1) * MLA_VDIM] = o.astype(o_ref.dtype)


def _mla_prompt_t(qcat, kcat, wuvt, geo):
    T = qcat.shape[0]
    nqb = geo.lp // TQ
    npb = geo.bp * nqb
    kern = functools.partial(_mla_prompt_t_kernel, nqb=nqb, n_prompt_blocks=npb)
    return pl.pallas_call(
        kern,
        grid=(T // TQ,),
        in_specs=[pl.BlockSpec((TQ, MLA_HEADS * MLA_QW), lambda qq: (qq, 0)),
                  pl.BlockSpec((geo.lp, MLA_QW), lambda qq: (_prompt_batch(qq, nqb, geo.bp), 0)),
                  pl.BlockSpec((LANES, MLA_QW), lambda qq: (geo.meta0 // LANES, 0)),
                  pl.BlockSpec(wuvt.shape, lambda qq: (0, 0, 0))],
        out_specs=pl.BlockSpec((TQ, MLA_HEADS * MLA_VDIM), lambda qq: (qq, 0)),
        out_shape=jax.ShapeDtypeStruct((T, MLA_HEADS * MLA_VDIM), BF16),
        scratch_shapes=[pltpu.VMEM((MLA_QW, MLA_HEADS * TQ), BF16),
                        pltpu.VMEM((MLA_KV_RANK, geo.lp), BF16),
                        pltpu.VMEM((MLA_KV_RANK, MLA_HEADS * TQ), F32)],
        compiler_params=_cparams(("arbitrary",)),
        name="mla_prompt",
    )(qcat, kcat, kcat, wuvt)


def _mla_prompt(qcat, kcat, wuv, geo):
    T = qcat.shape[0]
    nqb = geo.lp // TQ
    npb = geo.bp * nqb
    kern = functools.partial(_mla_prompt_kernel, nqb=nqb, n_prompt_blocks=npb)
    return pl.pallas_call(
        kern,
        grid=(T // TQ,),
        in_specs=[pl.BlockSpec((TQ, MLA_HEADS * MLA_QW), lambda qq: (qq, 0)),
                  pl.BlockSpec((geo.lp, MLA_QW), lambda qq: (_prompt_batch(qq, nqb, geo.bp), 0)),
                  pl.BlockSpec((N_META, MLA_QW), lambda qq: (geo.meta0 // N_META, 0)),
                  pl.BlockSpec(wuv.shape, lambda qq: (0, 0, 0))],
        out_specs=pl.BlockSpec((TQ, MLA_HEADS * MLA_VDIM), lambda qq: (qq, 0)),
        out_shape=jax.ShapeDtypeStruct((T, MLA_HEADS * MLA_VDIM), BF16),
        scratch_shapes=[pltpu.VMEM((MLA_HEADS * TQ, MLA_QW), BF16),
                        pltpu.VMEM((MLA_HEADS * TQ, 1), F32),
                        pltpu.VMEM((MLA_HEADS * TQ, 1), F32),
                        pltpu.VMEM((MLA_HEADS * TQ, MLA_KV_RANK), F32)],
        compiler_params=_cparams(("arbitrary",)),
        name="mla_prompt",
    )(qcat, kcat, kcat, wuv)


def _block_diag_queries(q, n_blocks, width):
    ls = q.shape[0]
    lane = lax.broadcasted_iota(jnp.int32, q.shape, 1)
    zero = jnp.zeros_like(q)
    return jnp.concatenate([jnp.where(lane // width == r, q, zero) for r in range(n_blocks)], axis=0)


def _expand_rows(c, ls):
    return jnp.concatenate([jnp.broadcast_to(c[r:r + 1, :], (ls, c.shape[1])) for r in range(c.shape[0])], axis=0)


def _three_part_softmax(parts):
    m = parts[0][0].max(axis=1, keepdims=True)
    for s, _ in parts[1:]:
        m = jnp.maximum(m, s.max(axis=1, keepdims=True))
    l = 0.0
    acc = 0.0
    for s, v in parts:
        p = jnp.exp(s - m)
        l = l + jnp.sum(p, axis=1, keepdims=True)
        acc = acc + _dot(p.astype(BF16), v)
    return acc / l


def _fox_sample_kernel(q_ref, kn_ref, vn_ref, km_ref, vm_ref, kp_ref, vp_ref,
                       cq_ref, ckm_ref, ckp_ref, ckn_ref, prev_ref, o_ref):
    del prev_ref
    ls = q_ref.shape[0]
    qbd = _block_diag_queries(q_ref[...], FOX_HEADS, HEAD_DIM)
    cq = cq_ref[...]
    rowc = jnp.concatenate([cq[:, r:r + 1] for r in range(FOX_HEADS)], axis=0)
    s_meta = _nt_dot(qbd, km_ref[...]) - _expand_rows(ckm_ref[...], ls) + rowc
    s_past = _nt_dot(qbd, kp_ref[0].astype(BF16)) - _expand_rows(ckp_ref[0], ls) + rowc
    s_new = _nt_dot(qbd, kn_ref[...]) - _expand_rows(ckn_ref[0], ls) + rowc
    rr, cc = _block_iotas(FOX_HEADS * ls, ls)
    s_new = jnp.where(cc <= rr % ls, s_new, NEG)
    obd = _three_part_softmax([(s_meta, vm_ref[...]), (s_past, vp_ref[0].astype(BF16)), (s_new, vn_ref[...])])
    lane = lax.broadcasted_iota(jnp.int32, (ls, FOX_W), 1)
    out = jnp.zeros((ls, FOX_W), F32)
    for r in range(FOX_HEADS):
        out = out + jnp.where(lane // HEAD_DIM == r, obd[r * ls:(r + 1) * ls, :], 0.0)
    o_ref[...] = out.astype(o_ref.dtype)


def _fox_sample(qf, kf, vf, past_k, past_v, c_col8, c_row_meta8, c_row_past, c_row_new, prev, geo):
    ls, bs, P = geo.ls, geo.bs, geo.past
    row0 = geo.sample0 // ls
    rmap = lambda s: (row0 + s, 0)
    mmap = lambda s: (geo.meta0 // N_META, 0)
    return pl.pallas_call(
        _fox_sample_kernel,
        grid=(bs,),
        in_specs=[pl.BlockSpec((ls, FOX_W), rmap), pl.BlockSpec((ls, FOX_W), rmap), pl.BlockSpec((ls, FOX_W), rmap),
                  pl.BlockSpec((N_META, FOX_W), mmap), pl.BlockSpec((N_META, FOX_W), mmap),
                  pl.BlockSpec((1, P, FOX_W), lambda s: (s, 0, 0)), pl.BlockSpec((1, P, FOX_W), lambda s: (s, 0, 0)),
                  pl.BlockSpec((ls, FOX_HEADS), rmap),
                  pl.BlockSpec((FOX_HEADS, N_META), lambda s: (0, 0)),
                  pl.BlockSpec((1, FOX_HEADS, P), lambda s: (s, 0, 0)),
                  pl.BlockSpec((1, FOX_HEADS, ls), lambda s: (s, 0, 0)),
                  pl.BlockSpec(memory_space=pl.ANY)],
        out_specs=pl.BlockSpec((ls, FOX_W), rmap),
        out_shape=jax.ShapeDtypeStruct(prev.shape, prev.dtype),
        input_output_aliases={11: 0},
        compiler_params=_cparams(("parallel",)),
        name="fox_sample",
    )(qf, kf, vf, kf, vf, past_k, past_v, c_col8, c_row_meta8, c_row_past, c_row_new, prev)


def _diff_sample_kernel(q_ref, kn_ref, vn_ref, km_ref, vm_ref, kp_ref, vp_ref, lam_ref, sub_ref, prev_ref, o_ref, *,
                        lambda_init, past):
    del prev_ref
    ls = q_ref.shape[0]
    nb = 2 * DIFF_HEADS
    qbd = _block_diag_queries(q_ref[...], nb, HEAD_DIM)
    s_meta = _nt_dot(qbd, km_ref[...])
    s_past = _nt_dot(qbd, kp_ref[0].astype(BF16))
    s_new = _nt_dot(qbd, kn_ref[...])
    rr, cc = _block_iotas(nb * ls, ls)
    s_new = jnp.where((past + cc) // CHUNK <= (past + rr % ls) // CHUNK, s_new, NEG)
    obd = _three_part_softmax([(s_meta, vm_ref[...]), (s_past, vp_ref[0].astype(BF16)), (s_new, vn_ref[...])])
    width = DIFF_HEADS * DIFF_VDIM
    lane = lax.broadcasted_iota(jnp.int32, (ls, width), 1)
    o0 = jnp.zeros((ls, width), F32)
    o1 = jnp.zeros((ls, width), F32)
    for hd in range(DIFF_HEADS):
        sel = lane // DIFF_VDIM == hd
        o0 = o0 + jnp.where(sel, obd[(2 * hd) * ls:(2 * hd + 1) * ls, :], 0.0)
        o1 = o1 + jnp.where(sel, obd[(2 * hd + 1) * ls:(2 * hd + 2) * ls, :], 0.0)
    o = o0 - _diff_lambda(lam_ref, lambda_init) * o1
    sub = sub_ref[...]
    segs = []
    for hd in range(DIFF_HEADS):
        segs.append(_rms(o[:, hd * DIFF_VDIM:(hd + 1) * DIFF_VDIM], sub) * (1.0 - lambda_init))
    o_ref[...] = jnp.concatenate(segs, axis=1).astype(o_ref.dtype)


def _diff_sample(qd, kd, vd, past_k, past_v, lam_p, subln, prev, geo, lambda_init):
    ls, bs, P = geo.ls, geo.bs, geo.past
    width = DIFF_HEADS * DIFF_VDIM
    row0 = geo.sample0 // ls
    rmap = lambda s: (row0 + s, 0)
    mmap = lambda s: (geo.meta0 // N_META, 0)
    full = lambda a: pl.BlockSpec(a.shape, lambda s: (0,) * a.ndim)
    kern = functools.partial(_diff_sample_kernel, lambda_init=lambda_init, past=P)
    return pl.pallas_call(
        kern,
        grid=(bs,),
        in_specs=[pl.BlockSpec((ls, width), rmap), pl.BlockSpec((ls, width), rmap), pl.BlockSpec((ls, width), rmap),
                  pl.BlockSpec((N_META, width), mmap), pl.BlockSpec((N_META, width), mmap),
                  pl.BlockSpec((1, P, width), lambda s: (s, 0, 0)), pl.BlockSpec((1, P, width), lambda s: (s, 0, 0)),
                  full(lam_p), full(subln),
                  pl.BlockSpec(memory_space=pl.ANY)],
        out_specs=pl.BlockSpec((ls, width), rmap),
        out_shape=jax.ShapeDtypeStruct(prev.shape, prev.dtype),
        input_output_aliases={9: 0},
        compiler_params=_cparams(("parallel",)),
        name="diff_sample",
    )(qd, kd, vd, kd, vd, past_k, past_v, lam_p, subln, prev)


def _mla_sample_kernel(q_ref, kn_ref, km_ref, cp_ref, rp_ref, wuv_ref, prev_ref, o_ref, *, past):
    del prev_ref
    ls = q_ref.shape[0]
    q = q_ref[...]
    qst = jnp.concatenate([q[:, hd * MLA_QW:(hd + 1) * MLA_QW] for hd in range(MLA_HEADS)], axis=0)
    ckv = cp_ref[0].astype(BF16)
    kro = rp_ref[0].astype(BF16)
    kn = kn_ref[...]
    km = km_ref[...]
    s_meta = _nt_dot(qst, km)
    s_past = (_nt_dot(qst[:, 0:MLA_KV_RANK], ckv)
              + _nt_dot(qst[:, MLA_KV_RANK:MLA_KV_RANK + MLA_ROPE], kro))
    s_new = _nt_dot(qst, kn)
    rr, cc = _block_iotas(MLA_HEADS * ls, ls)
    s_new = jnp.where((past + cc) // CHUNK <= (past + rr % ls) // CHUNK, s_new, NEG)
    olat = _three_part_softmax([(s_meta, km[:, 0:MLA_KV_RANK]), (s_past, ckv), (s_new, kn[:, 0:MLA_KV_RANK])])
    outs = []
    for hd in range(MLA_HEADS):
        outs.append(_dot(olat[hd * ls:(hd + 1) * ls, :].astype(BF16), wuv_ref[hd]))
    o_ref[...] = jnp.concatenate(outs, axis=1).astype(o_ref.dtype)


def _mla_sample(qcat, kcat, past_ckv, past_kr, wuv, prev, geo):
    ls, bs, P = geo.ls, geo.bs, geo.past
    row0 = geo.sample0 // ls
    rmap = lambda s: (row0 + s, 0)
    kern = functools.partial(_mla_sample_kernel, past=P)
    return pl.pallas_call(
        kern,
        grid=(bs,),
        in_specs=[pl.BlockSpec((ls, MLA_HEADS * MLA_QW), rmap),
                  pl.BlockSpec((ls, MLA_QW), rmap),
                  pl.BlockSpec((N_META, MLA_QW), lambda s: (geo.meta0 // N_META, 0)),
                  pl.BlockSpec((1, P, MLA_KV_RANK), lambda s: (s, 0, 0)),
                  pl.BlockSpec((1, P, MLA_ROPE), lambda s: (s, 0, 0)),
                  pl.BlockSpec(wuv.shape, lambda s: (0, 0, 0)),
                  pl.BlockSpec(memory_space=pl.ANY)],
        out_specs=pl.BlockSpec((ls, MLA_HEADS * MLA_VDIM), rmap),
        out_shape=jax.ShapeDtypeStruct(prev.shape, prev.dtype),
        input_output_aliases={6: 0},
        compiler_params=_cparams(("parallel",)),
        name="mla_sample",
    )(qcat, kcat, kcat, past_ckv, past_kr, wuv, prev)


def _post_attn_kernel(*refs, n_in):
    a_refs = refs[:n_in]
    w_refs = refs[n_in:2 * n_in]
    x_ref, g_ref, wr_ref, br_ref = refs[2 * n_in:2 * n_in + 4]
    x1_ref, xs_ref, gs_ref, lp_ref, cnt_ref = refs[2 * n_in + 4:]

    y = _dot(a_refs[0][...], w_refs[0][...])
    for a_ref, w_ref in zip(a_refs[1:], w_refs[1:]):
        y = y + _dot(a_ref[...], w_ref[...])
    x1 = x_ref[...] + y
    x1_ref[...] = x1
    h = _rms(x1, g_ref[...])

    logits = jnp.dot(h, wr_ref[...], precision=HIGHEST, preferred_element_type=F32) + br_ref[...]
    lane = lax.broadcasted_iota(jnp.int32, (TM, LANES), 1)
    lanef = lane.astype(F32)
    is_g = jnp.logical_and(lane >= N_EXPERTS, lane < N_EXPERTS + N_GROUPS)
    gl = jnp.where(is_g, logits, -jnp.inf)
    gmax = jnp.max(gl, axis=1, keepdims=True)
    gsel = jnp.min(jnp.where(gl == gmax, lanef - N_EXPERTS, 1e9), axis=1, keepdims=True)
    p_g = 1.0 / jnp.sum(jnp.exp(gl - gmax), axis=1, keepdims=True)
    lane_group = (lane // EXPERTS_PER_GROUP).astype(F32)
    in_group = jnp.logical_and(lane < N_EXPERTS, lane_group == gsel)
    el = jnp.where(in_group, logits, -jnp.inf)
    v1 = jnp.max(el, axis=1, keepdims=True)
    i1 = jnp.min(jnp.where(el == v1, lanef, 1e9), axis=1, keepdims=True)
    el2 = jnp.where(lanef == i1, -jnp.inf, el)
    v2 = jnp.max(el2, axis=1, keepdims=True)
    i2 = jnp.min(jnp.where(el2 == v2, lanef, 1e9), axis=1, keepdims=True)
    e2 = jnp.exp(v2 - v1)
    den = 1.0 + e2
    wa = (1.0 / den) * p_g
    wb = (e2 / den) * p_g
    j1 = i1 - EXPERTS_PER_GROUP * gsel
    j2 = i2 - EXPERTS_PER_GROUP * gsel
    gate = jnp.where(lanef == j1, wa, 0.0) + jnp.where(lanef == j2, wb, 0.0)

    ohg = jnp.where(lanef == gsel, 1.0, 0.0)
    counts = jnp.sum(ohg, axis=0, keepdims=True)
    rows16 = jnp.floor((counts + (ROW_CHUNK - 1)) * (1.0 / ROW_CHUNK)) * ROW_CHUNK
    r128, c128 = _block_iotas(LANES, LANES)
    before = jnp.where(r128 < c128, 1.0, 0.0).astype(BF16)
    off = _dot(jnp.broadcast_to(rows16, (8, LANES)).astype(BF16), before)[0:1, :]
    rt, ct = _block_iotas(TM, TM)
    earlier = jnp.where(ct < rt, 1.0, 0.0).astype(BF16)
    rank = _dot(earlier, ohg.astype(BF16))
    lp = jnp.sum(ohg * (off + rank), axis=1, keepdims=True)
    lp_b = jnp.broadcast_to(lp, (TM, LANES))
    lp_ref[...] = lp_b
    cnt_ref[0] = jnp.broadcast_to(counts, (8, LANES))
    lp_row = jnp.transpose(lp_b)[0:1, :]
    rloc = lax.broadcasted_iota(jnp.int32, (LOCAL_ROWS, TM), 0).astype(F32)
    perm = jnp.where(rloc == lp_row, 1.0, 0.0)
    xs_ref[...] = _dot(perm.astype(BF16), h.astype(BF16)).astype(BF16)
    gs_ref[...] = jnp.dot(perm, gate, precision=HIGHEST, preferred_element_type=F32)


def _post_attn(attn_list, w_list, x, g, wr, br):
    T = x.shape[0]
    nt = T // TM
    row = lambda w: pl.BlockSpec((TM, w), lambda i: (i, 0))
    full = lambda a: pl.BlockSpec(a.shape, lambda i: (0,) * a.ndim)
    kern = functools.partial(_post_attn_kernel, n_in=len(attn_list))
    outs = [jax.ShapeDtypeStruct((T, x.shape[1]), F32),
            jax.ShapeDtypeStruct((nt * LOCAL_ROWS, x.shape[1]), BF16),
            jax.ShapeDtypeStruct((nt * LOCAL_ROWS, LANES), F32),
            jax.ShapeDtypeStruct((T, LANES), F32),
            jax.ShapeDtypeStruct((nt, 8, LANES), F32)]
    out_specs = [row(x.shape[1]),
                 pl.BlockSpec((LOCAL_ROWS, x.shape[1]), lambda i: (i, 0)),
                 pl.BlockSpec((LOCAL_ROWS, LANES), lambda i: (i, 0)),
                 row(LANES),
                 pl.BlockSpec((1, 8, LANES), lambda i: (i, 0, 0))]
    return pl.pallas_call(
        kern,
        grid=(nt,),
        in_specs=([row(a.shape[1]) for a in attn_list] + [full(w) for w in w_list]
                  + [row(x.shape[1]), full(g), full(wr), full(br)]),
        out_specs=out_specs,
        out_shape=outs,
        compiler_params=_cparams(("parallel",)),
        name="post_attn_route",
    )(*attn_list, *w_list, x, g, wr, br)


def _chunk_tables(cnt, n_tiles):
    nt = cnt.shape[0]
    chunks = jnp.ceil(cnt / ROW_CHUNK).astype(jnp.int32)
    cum = jnp.cumsum(chunks, axis=1)
    c = jnp.arange(LOCAL_CHUNKS, dtype=jnp.int32)
    keys = jnp.sum(c[None, :, None] >= cum[:, None, :], axis=2).reshape(-1)
    order = jnp.argsort(keys, stable=True).astype(jnp.int32)
    ng = N_GROUPS + 1
    n_g = jnp.sum(keys[:, None] == jnp.arange(ng)[None, :], axis=0).astype(jnp.int32)
    tiles_g = (n_g + EXPERT_TILE_CHUNKS - 1) // EXPERT_TILE_CHUNKS
    tile_end = jnp.cumsum(tiles_g)
    start_slot = (tile_end - tiles_g) * EXPERT_TILE_CHUNKS
    start_sorted = jnp.cumsum(n_g) - n_g
    gj = keys[order]
    j = jnp.arange(nt * LOCAL_CHUNKS, dtype=jnp.int32)
    slot = start_slot[gj] + (j - start_sorted[gj])
    src = jnp.full((n_tiles * EXPERT_TILE_CHUNKS,), -1, jnp.int32).at[slot].set(order)
    wgrp = jnp.sum(jnp.arange(n_tiles)[:, None] >= tile_end[None, :], axis=1)
    wgrp = jnp.minimum(wgrp, N_GROUPS - 1).astype(jnp.int32)
    return src, wgrp


def _experts_kernel(src_ref, wgrp_ref, xs_hbm, gs_hbm, wg_ref, wu_ref, wd_ref, ys_hbm,
                    xbuf, gbuf, obuf, sem_x, sem_g, sem_o):
    del wgrp_ref
    i = pl.program_id(0)
    base = i * EXPERT_TILE_CHUNKS

    def chunk_rows(cid):
        return pl.ds(pl.multiple_of(cid * ROW_CHUNK, ROW_CHUNK), ROW_CHUNK)

    def in_copies(c):
        sid = src_ref[base + c]
        rid = jnp.where(sid < 0, LOCAL_CHUNKS - 1, sid)
        dst = pl.ds(c * ROW_CHUNK, ROW_CHUNK)
        return (pltpu.make_async_copy(xs_hbm.at[chunk_rows(rid), :], xbuf.at[dst, :], sem_x.at[0]),
                pltpu.make_async_copy(gs_hbm.at[chunk_rows(rid), :], gbuf.at[dst, :], sem_g.at[0]))

    def out_copy(c):
        sid = src_ref[base + c]
        rid = jnp.maximum(sid, 0)
        return sid >= 0, pltpu.make_async_copy(obuf.at[pl.ds(c * ROW_CHUNK, ROW_CHUNK), :],
                                               ys_hbm.at[chunk_rows(rid), :], sem_o.at[0])

    for c in range(EXPERT_TILE_CHUNKS):
        cx, cg = in_copies(c)
        cx.start()
        cg.start()
    for c in range(EXPERT_TILE_CHUNKS):
        cx, cg = in_copies(c)
        cx.wait()
        cg.wait()

    x = xbuf[...]
    a = _dot(x, wg_ref[...])
    b = _dot(x, wu_ref[...])
    gate = gbuf[...]
    parts = []
    for e in range(EXPERTS_PER_GROUP):
        sl = slice(e * EXPERT_FF, (e + 1) * EXPERT_FF)
        ae = a[:, sl]
        parts.append((ae * jax.nn.sigmoid(ae) * b[:, sl] * gate[:, e:e + 1]).astype(BF16))
    hdn = jnp.concatenate(parts, axis=1)
    obuf[...] = _dot(hdn, wd_ref[...])

    for c in range(EXPERT_TILE_CHUNKS):
        valid, co = out_copy(c)

        @pl.when(valid)
        def _():
            co.start()
    for c in range(EXPERT_TILE_CHUNKS):
        valid, co = out_copy(c)

        @pl.when(valid)
        def _():
            co.wait()


def _experts(xs, gs, src, wgrp, wg, wu, wd):
    n_tiles = wgrp.shape[0]
    d = xs.shape[1]
    gw = EXPERTS_PER_GROUP * EXPERT_FF
    grid_spec = pltpu.PrefetchScalarGridSpec(
        num_scalar_prefetch=2,
        grid=(n_tiles,),
        in_specs=[pl.BlockSpec(memory_space=pl.ANY),
                  pl.BlockSpec(memory_space=pl.ANY),
                  pl.BlockSpec((None, d, gw), lambda i, src, wgrp: (wgrp[i], 0, 0)),
                  pl.BlockSpec((None, d, gw), lambda i, src, wgrp: (wgrp[i], 0, 0)),
                  pl.BlockSpec((None, gw, d), lambda i, src, wgrp: (wgrp[i], 0, 0))],
        out_specs=pl.BlockSpec(memory_space=pl.ANY),
        scratch_shapes=[pltpu.VMEM((TM, d), BF16), pltpu.VMEM((TM, LANES), F32), pltpu.VMEM((TM, d), F32),
                        pltpu.SemaphoreType.DMA((1,)), pltpu.SemaphoreType.DMA((1,)), pltpu.SemaphoreType.DMA((1,))],
    )
    return pl.pallas_call(
        _experts_kernel,
        grid_spec=grid_spec,
        out_shape=jax.ShapeDtypeStruct((xs.shape[0], d), F32),
        compiler_params=_cparams(("arbitrary",)),
        name="moe_experts",
    )(src, wgrp, xs, gs, wg, wu, wd)


def _moe_combine_kernel(x1_ref, ys_ref, lp_ref, gfin_ref, o_ref, *, final):
    d = x1_ref.shape[1]
    yl = jnp.concatenate([ys_ref[...], jnp.zeros((LOCAL_ROWS_PAD - LOCAL_ROWS, d), F32)], axis=0)
    lp = lp_ref[...][:, 0:1]
    lanef = lax.broadcasted_iota(jnp.int32, (TM, LOCAL_ROWS_PAD), 1).astype(F32)
    sel = jnp.where(lanef == lp, 1.0, 0.0).astype(BF16)
    hi = yl.astype(BF16)
    r1 = yl - hi.astype(F32)
    mid = r1.astype(BF16)
    lo = (r1 - mid.astype(F32)).astype(BF16)
    moe = (_dot(sel, hi) + _dot(sel, mid)) + _dot(sel, lo)
    x2 = x1_ref[...] + moe
    if final:
        x2 = _rms(x2, gfin_ref[...])
    o_ref[...] = x2


def _moe_combine(x1, ys, lp, gfin, final):
    T, d = x1.shape
    row = lambda w: pl.BlockSpec((TM, w), lambda i: (i, 0))
    kern = functools.partial(_moe_combine_kernel, final=final)
    return pl.pallas_call(
        kern,
        grid=(T // TM,),
        in_specs=[row(d), pl.BlockSpec((LOCAL_ROWS, d), lambda i: (i, 0)), row(LANES),
                  pl.BlockSpec(gfin.shape, lambda i: (0, 0))],
        out_specs=row(d),
        out_shape=jax.ShapeDtypeStruct((T, d), F32),
        compiler_params=_cparams(("parallel",)),
        name="moe_combine",
    )(x1, ys, lp, gfin)


class _Geometry:
    def __init__(self, bp, lp, bs, ls, past):
        self.bp, self.lp, self.bs, self.ls, self.past = bp, lp, bs, ls, past
        self.prompt_rows = bp * lp
        self.sample0 = self.prompt_rows
        self.sample_rows = bs * ls
        self.meta0 = self.prompt_rows + self.sample_rows
        self.total = self.meta0 + TM
        assert lp % TQ == 0 and self.sample_rows % TM == 0 and ls % ROW_CHUNK == 0


def _rotary_tables(geo):
    pos = np.zeros((geo.total,), np.int32)
    pos[:geo.prompt_rows] = np.tile(N_META + np.arange(geo.lp), geo.bp)
    pos[geo.sample0:geo.meta0] = np.tile(N_META + geo.past + np.arange(geo.ls), geo.bs)
    pos[geo.meta0:geo.meta0 + N_META] = np.arange(N_META)
    inv = ROPE_THETA ** (-jnp.arange(0, HEAD_DIM, 2, dtype=F32) / HEAD_DIM)
    ang = jnp.asarray(pos).astype(F32)[:, None] * inv[None, :]
    cos = jnp.concatenate([jnp.cos(ang), jnp.cos(ang)], -1)
    sin = jnp.concatenate([jnp.sin(ang), jnp.sin(ang)], -1)
    zeros = jnp.zeros_like(cos)
    return (jnp.concatenate([cos, cos], 1), jnp.concatenate([sin, sin], 1),
            jnp.concatenate([cos, zeros], 1), jnp.concatenate([sin, zeros], 1))


def _keep_bf16_bits(x):
    bits = lax.bitcast_convert_type(x, jnp.uint32) & jnp.uint32(0xFFFF0000)
    return lax.bitcast_convert_type(bits, F32)


def _rot_cols(w, group):
    shp = w.shape
    wg = w.reshape(shp[:-1] + (shp[-1] // group, 2, group // 2))
    return jnp.concatenate([-wg[..., 1:2, :], wg[..., 0:1, :]], axis=-2).reshape(shp)


def _moe_layer(attn_list, w_list, x, g_ffn, wr, br, wg, wu, wd, gfin, final):
    nt = x.shape[0] // TM
    n_tiles = -(-(nt * LOCAL_CHUNKS) // EXPERT_TILE_CHUNKS) + N_GROUPS + 1
    x1, xs, gs, lp, cnt = _post_attn(attn_list, w_list, x, g_ffn, wr, br)
    src, wgrp = _chunk_tables(cnt[:, 0, :N_GROUPS], n_tiles)
    ys = _experts(xs, gs, src, wgrp, wg, wu, wd)
    return _moe_combine(x1, ys, lp, gfin, final)


def _router_weights(w_group, b_group, w_router, b_router):
    d = w_group.shape[0]
    wr = jnp.zeros((d, LANES), F32).at[:, :N_EXPERTS].set(w_router).at[:, N_EXPERTS:N_EXPERTS + N_GROUPS].set(w_group)
    br = jnp.zeros((1, LANES), F32).at[0, :N_EXPERTS].set(b_router).at[0, N_EXPERTS:N_EXPERTS + N_GROUPS].set(b_group)
    return wr, br


def _expert_weights(w_gate, w_up, w_down):
    d = w_gate.shape[1]

    def cat(w):
        w = w.reshape(N_GROUPS, EXPERTS_PER_GROUP, d, EXPERT_FF).transpose(0, 2, 1, 3)
        return w.reshape(N_GROUPS, d, EXPERTS_PER_GROUP * EXPERT_FF).astype(BF16)

    return cat(w_gate), cat(w_up), w_down.reshape(N_GROUPS, EXPERTS_PER_GROUP * EXPERT_FF, d).astype(BF16)


def kernel(x_prompt, x_sample, cache_fox_k, cache_fox_v, cache_fox_logf, cache_diff_k, cache_diff_v, cache_mla_ckv, cache_mla_krope, meta_tokens, norm_mix, norm_ffn, norm_final, w_in_even, fox_b_f, diff_lambda, diff_subln, w_out_even, w_in_odd, mla_norm_q, mla_norm_kv, mla_w_uq, mla_w_uk, mla_w_uv, w_out_odd, moe_w_group, moe_b_group, moe_w_router, moe_b_router, moe_w_gate, moe_w_up, moe_w_down):
    bp, lp, d = x_prompt.shape
    bs, ls, _ = x_sample.shape
    past = cache_fox_k.shape[2]
    geo = _Geometry(bp, lp, bs, ls, past)
    T, PR, SR, M0 = geo.total, geo.prompt_rows, geo.sample_rows, geo.meta0
    cos2, sin2, cosp, sinp = _rotary_tables(geo)

    x = jnp.concatenate([x_prompt.reshape(PR, d), x_sample.reshape(SR, d), meta_tokens,
                         jnp.zeros((TM - N_META, d), F32)], axis=0)

    w = w_in_even[0]
    o = 3 * FOX_W
    wq_d = w[:, o + FOX_HEADS:o + FOX_HEADS + 512]
    wk_d = w[:, o + FOX_HEADS + 512:o + FOX_HEADS + 1024]
    wv_d = w[:, o + FOX_HEADS + 1024:o + FOX_HEADS + 1536]
    wf = jnp.zeros((d, LANES), F32).at[:, :FOX_HEADS].set(w[:, o:o + FOX_HEADS])
    w_all = jnp.concatenate([w[:, 0:o], wf, wq_d, _rot_cols(wq_d, HEAD_DIM), wk_d, _rot_cols(wk_d, HEAD_DIM), wv_d],
                            axis=1).astype(BF16)
    b_f = jnp.zeros((1, LANES), F32).at[0, :FOX_HEADS].set(fox_b_f[0])
    (qf, kf, vf, qd, kd, vd, kf32, vf32, lf, kd32, vd32) = _proj_even(
        x, norm_mix[0][None, :], w_all, b_f, cos2, sin2)

    lf8 = lf[:, :FOX_HEADS]
    meta_lf = lf8[M0:M0 + N_META]
    rows_p = jnp.concatenate([jnp.broadcast_to(meta_lf[None], (bp, N_META, FOX_HEADS)),
                              lf8[:PR].reshape(bp, lp, FOX_HEADS)], axis=1)
    rows_p = rows_p.transpose(0, 2, 1).reshape(bp * FOX_HEADS, N_META + lp)
    rows_s = jnp.concatenate([jnp.broadcast_to(meta_lf[None], (bs, N_META, FOX_HEADS)),
                              cache_fox_logf[0], lf8[PR:PR + SR].reshape(bs, ls, FOX_HEADS)], axis=1)
    rows_s = rows_s.transpose(0, 2, 1).reshape(bs * FOX_HEADS, N_META + past + ls)
    c_p = _cumsum_rows(rows_p)
    c_s = _cumsum_rows(rows_s)
    c_meta_row = c_p[:FOX_HEADS, :N_META]
    c_prompt = c_p[:, N_META:N_META + lp].reshape(bp, FOX_HEADS, lp)
    c_past = c_s[:, N_META:N_META + past].reshape(bs, FOX_HEADS, past)
    c_new = c_s[:, N_META + past:N_META + past + ls].reshape(bs, FOX_HEADS, ls)
    c_col8 = jnp.concatenate([c_prompt.transpose(0, 2, 1).reshape(PR, FOX_HEADS),
                              c_new.transpose(0, 2, 1).reshape(SR, FOX_HEADS),
                              c_meta_row.T, jnp.zeros((TM - N_META, FOX_HEADS), F32)], axis=0)
    c_col = c_col8.reshape(T, FOX_HEADS // 2, 2).transpose(1, 0, 2)
    c_row = c_prompt.reshape(bp, FOX_HEADS // 2, 2, lp)
    c_row_meta = c_meta_row.reshape(FOX_HEADS // 2, 2, N_META)

    lambda_init = 0.8 - 0.6 * math.exp(-0.3 * 0)
    lam_p = diff_lambda[0]
    subln = diff_subln[0][None, :]
    c_hi = _keep_bf16_bits(c_col8)
    c_mid = _keep_bf16_bits(c_col8 - c_hi)
    c_lo = (c_col8 - c_hi) - c_mid
    c3 = jnp.stack([c_hi, c_mid, c_lo], axis=-1).astype(BF16).reshape(T, N_PAIRS, 2 * BIAS_LANES)
    c3 = jnp.pad(c3, ((0, 0), (0, 0), (0, LANES - 2 * BIAS_LANES))).reshape(T, FOX_W)
    cq_row = c_col8.reshape(T // TQ, TQ, FOX_HEADS).transpose(0, 2, 1).reshape(T // TQ, 1, FOX_HEADS * TQ)
    fox_o = _pair_attn_t(qf, kf, vf, (c3, cq_row), geo, True)
    fox_o = _fox_sample(qf, kf, vf, cache_fox_k[0].reshape(bs, past, FOX_W), cache_fox_v[0].reshape(bs, past, FOX_W),
                        c_col8, c_meta_row, c_past, c_new, fox_o, geo)
    diff_o = _pair_attn_t(qd, kd, vd, (lam_p, diff_subln[0][:, None]), geo, False, lambda_init)
    diff_o = _diff_sample(qd, kd, vd, cache_diff_k[0].reshape(bs, past, 512), cache_diff_v[0].reshape(bs, past, 512),
                          lam_p, subln, diff_o, geo, lambda_init)

    wo = w_out_even[0].astype(BF16)
    wr, br = _router_weights(moe_w_group[0], moe_b_group[0], moe_w_router[0], moe_b_router[0])
    wg, wu, wd = _expert_weights(moe_w_gate[0], moe_w_up[0], moe_w_down[0])
    x = _moe_layer([fox_o, diff_o], [wo[:FOX_W], wo[FOX_W:]], x, norm_ffn[0][None, :], wr, br, wg, wu, wd,
                   norm_final[None, :], False)

    w = w_in_odd[0]
    o = MLA_Q_RANK + MLA_KV_RANK
    wkr = w[:, o:o + MLA_ROPE]
    z64 = jnp.zeros((d, MLA_ROPE), F32)
    w_all = jnp.concatenate([w[:, :o], wkr, z64, _rot_cols(wkr, MLA_ROPE), z64], axis=1).astype(BF16)
    uq = mla_w_uq[0]
    uq_rope = uq[:, :, MLA_NOPE:]
    zr = jnp.zeros_like(uq_rope)
    wuq = jnp.concatenate([uq[:, :, :MLA_NOPE].reshape(MLA_Q_RANK, -1),
                           jnp.concatenate([uq_rope, zr], -1).reshape(MLA_Q_RANK, -1),
                           jnp.concatenate([_rot_cols(uq_rope, MLA_ROPE), zr], -1).reshape(MLA_Q_RANK, -1)],
                          axis=1).astype(BF16)
    wuk = mla_w_uk[0].transpose(1, 2, 0).astype(BF16)
    wuv = mla_w_uv[0].transpose(1, 0, 2).astype(BF16)
    qcat, kcat, ckv32, kr32 = _proj_odd(x, norm_mix[1][None, :], w_all, mla_norm_q[0][None, :],
                                        mla_norm_kv[0][None, :], wuq, wuk, cosp, sinp)
    mla_o = _mla_prompt_t(qcat, kcat, mla_w_uv[0].transpose(1, 2, 0).astype(BF16), geo)
    mla_o = _mla_sample(qcat, kcat, cache_mla_ckv[0], cache_mla_krope[0], wuv, mla_o, geo)

    wr, br = _router_weights(moe_w_group[1], moe_b_group[1], moe_w_router[1], moe_b_router[1])
    wg, wu, wd = _expert_weights(moe_w_gate[1], moe_w_up[1], moe_w_down[1])
    y = _moe_layer([mla_o], [w_out_odd[0].astype(BF16)], x, norm_ffn[1][None, :], wr, br, wg, wu, wd,
                   norm_final[None, :], True)

    def prompt_rows(a):
        w_ = a.shape[1]
        meta = jnp.broadcast_to(a[M0:M0 + N_META][None], (bp, N_META, w_))
        return jnp.concatenate([meta, a[:PR].reshape(bp, lp, w_)], axis=1)[None]

    def sample_rows(a):
        return a[PR:PR + SR].reshape(1, bs, ls, a.shape[1])

    lf8 = lf[:, :FOX_HEADS]
    kr = kr32[:, :MLA_ROPE]
    y_prompt = y[:PR].reshape(bp, lp, d)
    y_sample = y[PR:PR + SR].reshape(bs, ls, d)
    L = N_META + lp
    return (y_prompt, y_sample,
            prompt_rows(kf32).reshape(1, bp, L, FOX_HEADS, HEAD_DIM),
            prompt_rows(vf32).reshape(1, bp, L, FOX_HEADS, HEAD_DIM),
            prompt_rows(lf8),
            prompt_rows(kd32).reshape(1, bp, L, DIFF_HEADS, 2, HEAD_DIM),
            prompt_rows(vd32).reshape(1, bp, L, DIFF_HEADS, DIFF_VDIM),
            prompt_rows(ckv32), prompt_rows(kr),
            sample_rows(kf32).reshape(1, bs, ls, FOX_HEADS, HEAD_DIM),
            sample_rows(vf32).reshape(1, bs, ls, FOX_HEADS, HEAD_DIM),
            sample_rows(lf8),
            sample_rows(kd32).reshape(1, bs, ls, DIFF_HEADS, 2, HEAD_DIM),
            sample_rows(vd32).reshape(1, bs, ls, DIFF_HEADS, DIFF_VDIM),
            sample_rows(ckv32), sample_rows(kr))
```

```python
import functools
import math

import numpy as np
import jax
import jax.numpy as jnp
from jax import lax
from jax.experimental import pallas as pl
from jax.experimental.pallas import tpu as pltpu

F32 = jnp.float32
BF16 = jnp.bfloat16
HIGHEST = lax.Precision.HIGHEST

CHUNK = 64
N_META = 16
HEAD_DIM = 64
FOX_HEADS = 8
DIFF_HEADS = 4
DIFF_VDIM = 128
ROPE_THETA = 10000.0
MLA_HEADS = 8
MLA_Q_RANK = 384
MLA_KV_RANK = 256
MLA_NOPE = 128
MLA_ROPE = 64
MLA_VDIM = 128
N_GROUPS = 4
EXPERTS_PER_GROUP = 8
N_EXPERTS = N_GROUPS * EXPERTS_PER_GROUP
EXPERT_FF = 256
RMS_EPS = 1e-6
FOX_W = FOX_HEADS * HEAD_DIM
MLA_QW = MLA_KV_RANK + 128

LANES = 128
TM = 256
TQ = 256
N_PAIRS = 4
BIAS_LANES = 3
ROW_CHUNK = 16
LOCAL_CHUNKS = 21
LOCAL_ROWS = LOCAL_CHUNKS * ROW_CHUNK
LOCAL_ROWS_PAD = 384
EXPERT_TILE_CHUNKS = TM // ROW_CHUNK
NEG = -1e30
VMEM_LIMIT = 56 * 1024 * 1024


def _cparams(sem):
    return pltpu.CompilerParams(dimension_semantics=sem, vmem_limit_bytes=VMEM_LIMIT)


def _rms(x, g):
    ms = jnp.mean(x * x, axis=-1, keepdims=True)
    return (x * lax.rsqrt(ms + RMS_EPS)) * g


def _nt_dot(a, b):
    return lax.dot_general(a, b, (((1,), (1,)), ((), ())), preferred_element_type=F32)


def _dot(a, b):
    return jnp.dot(a, b, preferred_element_type=F32)


def _block_iotas(rows, cols):
    r = lax.broadcasted_iota(jnp.int32, (rows, cols), 0)
    c = lax.broadcasted_iota(jnp.int32, (rows, cols), 1)
    return r, c


def _transpose_bf16(x):
    return jnp.transpose(x.astype(F32)).astype(BF16)


def _write_state_rows(i, vals, stage_ref, sem_ref, outp_refs, outs_refs, *, bp, tiles_per_batch, npt, nst):
    n_arrays = len(vals)
    last = npt + nst
    slot = i % 2

    def prompt_copies(j, s):
        b = j // tiles_per_batch
        r0 = pl.multiple_of(N_META + (j % tiles_per_batch) * TM, 8)
        return [pltpu.make_async_copy(stage_ref.at[s, a], outp_refs[a].at[b, pl.ds(r0, TM), :], sem_ref.at[s, a])
                for a in range(n_arrays)]

    def sample_copies(j, s):
        r0 = pl.multiple_of((j - npt) * TM, TM)
        return [pltpu.make_async_copy(stage_ref.at[s, a], outs_refs[a].at[pl.ds(r0, TM), :], sem_ref.at[s, a])
                for a in range(n_arrays)]

    def meta_copies(s):
        return [pltpu.make_async_copy(stage_ref.at[s, a, 0:N_META, :], outp_refs[a].at[b, 0:N_META, :], sem_ref.at[s, a])
                for a in range(n_arrays) for b in range(bp)]

    def for_frame_tile(j, s, wait):
        @pl.when(j < npt)
        def _():
            for c in prompt_copies(j, s):
                c.wait() if wait else c.start()

        @pl.when(jnp.logical_and(j >= npt, j < last))
        def _():
            for c in sample_copies(j, s):
                c.wait() if wait else c.start()

    @pl.when(i >= 2)
    def _():
        for_frame_tile(i - 2, slot, True)

    for a, v in enumerate(vals):
        stage_ref[slot, a] = v
    for_frame_tile(i, slot, False)

    @pl.when(i == last)
    def _():
        for c in meta_copies(slot):
            c.start()
        for c in meta_copies(slot):
            c.wait()
        for_frame_tile(i - 1, 1 - slot, True)


def _state_out_shapes(geo, width, n):
    shapes = [jax.ShapeDtypeStruct((geo.bp, N_META + geo.lp, width), F32)] * n
    shapes += [jax.ShapeDtypeStruct((geo.sample_rows, width), F32)] * n
    return shapes


def _proj_even_kernel(x_ref, g_ref, w_ref, bf_ref, cs_ref, sn_ref,
                      qf_ref, kf_ref, vf_ref, qd_ref, kd_ref, vd_ref, lf_ref,
                      kfp, vfp, kdp, vdp, kfs, vfs, kds, vds, stage_ref, sem_ref, *, geo):
    h = _rms(x_ref[...], g_ref[...]).astype(BF16)

    def mm(i, width=FOX_W):
        return _dot(h, w_ref[:, i:i + width])

    scale = HEAD_DIM ** -0.5
    qf = mm(0)
    kf = mm(512)
    vf = mm(1024)
    fl = mm(1536, LANES)
    qd = mm(1664)
    qdr = mm(2176)
    kd = mm(2688)
    kdr = mm(3200)
    vd = mm(3712)
    cos = jnp.concatenate([cs_ref[...]] * 4, axis=1)
    sin = jnp.concatenate([sn_ref[...]] * 4, axis=1)
    qd = qd * cos + qdr * sin
    kd = kd * cos + kdr * sin
    z = fl + bf_ref[...]
    lf_ref[...] = jnp.minimum(z, 0.0) - jnp.log1p(jnp.exp(-jnp.abs(z)))
    qf_ref[...] = (qf * scale).astype(BF16)
    kf_ref[...] = kf.astype(BF16)
    vf_ref[...] = vf.astype(BF16)
    qd_ref[...] = (qd * scale).astype(BF16)
    kd_ref[...] = kd.astype(BF16)
    vd_ref[...] = vd.astype(BF16)
    _write_state_rows(pl.program_id(0), [kf, vf, kd, vd], stage_ref, sem_ref, [kfp, vfp, kdp, vdp],
                      [kfs, vfs, kds, vds], bp=geo.bp, tiles_per_batch=geo.lp // TM,
                      npt=geo.prompt_rows // TM, nst=geo.sample_rows // TM)


def _proj_even(x, g, w_all, b_f, cos2, sin2, geo):
    T = x.shape[0]
    row = lambda w: pl.BlockSpec((TM, w), lambda i: (i, 0))
    full = lambda a: pl.BlockSpec(a.shape, lambda i: (0,) * a.ndim)
    hbm = pl.BlockSpec(memory_space=pl.ANY)
    outs = ([jax.ShapeDtypeStruct((T, FOX_W), BF16)] * 6 + [jax.ShapeDtypeStruct((T, LANES), F32)]
            + _state_out_shapes(geo, FOX_W, 4))
    return pl.pallas_call(
        functools.partial(_proj_even_kernel, geo=geo),
        grid=(T // TM,),
        in_specs=[row(x.shape[1]), full(g), full(w_all), full(b_f), row(LANES), row(LANES)],
        out_specs=[row(FOX_W)] * 6 + [row(LANES)] + [hbm] * 8,
        out_shape=outs,
        scratch_shapes=[pltpu.VMEM((2, 4, TM, FOX_W), F32), pltpu.SemaphoreType.DMA((2, 4))],
        compiler_params=_cparams(("arbitrary",)),
        name="proj_even",
    )(x, g, w_all, b_f, cos2, sin2)


def _proj_odd_kernel(x_ref, g_ref, w_ref, gq_ref, gkv_ref, wuq_ref, wuk_ref, cs_ref, sn_ref,
                     q_ref, kcat_ref, kr32_ref, ckvp, ckvs, stage_ref, sem_ref, *, geo):
    h = _rms(x_ref[...], g_ref[...]).astype(BF16)
    scale = (MLA_NOPE + MLA_ROPE) ** -0.5
    cq = _dot(h, w_ref[:, 0:MLA_Q_RANK])
    ckv = _dot(h, w_ref[:, MLA_Q_RANK:MLA_Q_RANK + MLA_KV_RANK])
    kr = _dot(h, w_ref[:, 640:768])
    krr = _dot(h, w_ref[:, 768:896])
    cos = cs_ref[...]
    sin = sn_ref[...]
    kr = kr * cos + krr * sin
    ckv = _rms(ckv, gkv_ref[...])
    kr32_ref[...] = kr
    kcat_ref[...] = jnp.concatenate([ckv, kr], axis=1).astype(BF16)
    cqn = _rms(cq, gq_ref[...]).astype(BF16)
    hw = MLA_HEADS * LANES
    q_nope = _dot(cqn, wuq_ref[:, 0:hw])
    q_rope = _dot(cqn, wuq_ref[:, hw:2 * hw])
    q_rope_r = _dot(cqn, wuq_ref[:, 2 * hw:3 * hw])
    cos8 = jnp.concatenate([cos] * MLA_HEADS, axis=1)
    sin8 = jnp.concatenate([sin] * MLA_HEADS, axis=1)
    q_rope = (q_rope * cos8 + q_rope_r * sin8) * scale
    pieces = []
    for hd in range(MLA_HEADS):
        qn = q_nope[:, hd * LANES:(hd + 1) * LANES].astype(BF16)
        pieces.append((_dot(qn, wuk_ref[hd]) * scale).astype(BF16))
        pieces.append(q_rope[:, hd * LANES:(hd + 1) * LANES].astype(BF16))
    q_ref[...] = jnp.concatenate(pieces, axis=1)
    _write_state_rows(pl.program_id(0), [ckv], stage_ref, sem_ref, [ckvp], [ckvs], bp=geo.bp,
                      tiles_per_batch=geo.lp // TM, npt=geo.prompt_rows // TM, nst=geo.sample_rows // TM)


def _proj_odd(x, g, w_all, gq, gkv, wuq, wuk, cosp, sinp, geo):
    T = x.shape[0]
    row = lambda w: pl.BlockSpec((TM, w), lambda i: (i, 0))
    full = lambda a: pl.BlockSpec(a.shape, lambda i: (0,) * a.ndim)
    hbm = pl.BlockSpec(memory_space=pl.ANY)
    outs = [jax.ShapeDtypeStruct((T, MLA_HEADS * MLA_QW), BF16),
            jax.ShapeDtypeStruct((T, MLA_QW), BF16),
            jax.ShapeDtypeStruct((T, LANES), F32)] + _state_out_shapes(geo, MLA_KV_RANK, 1)
    return pl.pallas_call(
        functools.partial(_proj_odd_kernel, geo=geo),
        grid=(T // TM,),
        in_specs=[row(x.shape[1]), full(g), full(w_all), full(gq), full(gkv), full(wuq), full(wuk),
                  row(LANES), row(LANES)],
        out_specs=[row(MLA_HEADS * MLA_QW), row(MLA_QW), row(LANES), hbm, hbm],
        out_shape=outs,
        scratch_shapes=[pltpu.VMEM((2, 1, TM, MLA_KV_RANK), F32), pltpu.SemaphoreType.DMA((2, 1))],
        compiler_params=_cparams(("arbitrary",)),
        name="proj_odd",
    )(x, g, w_all, gq, gkv, wuq, wuk, cosp, sinp)


def _cumsum_kernel(x_ref, o_ref):
    rows, length = x_ref.shape
    r, c = _block_iotas(LANES, LANES)
    tri = jnp.where(r <= c, 1.0, 0.0).astype(BF16)

    def body(i, carry):
        off = pl.multiple_of(i * LANES, LANES)
        blk = x_ref[:, pl.ds(off, LANES)]
        hi = blk.astype(BF16)
        r1 = blk - hi.astype(F32)
        mid = r1.astype(BF16)
        lo = (r1 - mid.astype(F32)).astype(BF16)
        cs = (_dot(hi, tri) + _dot(mid, tri)) + _dot(lo, tri) + carry
        o_ref[:, pl.ds(off, LANES)] = cs
        return cs[:, LANES - 1:LANES]

    lax.fori_loop(0, length // LANES, body, jnp.zeros((rows, 1), F32))


def _cumsum_rows(x):
    pad = (-x.shape[1]) % LANES
    xp = jnp.pad(x, ((0, 0), (0, pad)))
    return pl.pallas_call(
        _cumsum_kernel,
        out_shape=jax.ShapeDtypeStruct(xp.shape, F32),
        compiler_params=pltpu.CompilerParams(vmem_limit_bytes=VMEM_LIMIT),
        name="logf_cumsum",
    )(xp)


def _frame_blocks(qq, nqb, n_prompt_blocks):
    tail = qq >= n_prompt_blocks
    return jnp.where(tail, 0, qq % nqb), jnp.where(tail, 0, qq % nqb + 1)


def _prompt_batch(qq, nqb, n_batch):
    return jnp.minimum(qq // nqb, n_batch - 1)


def _diff_lambda(lam_ref, lambda_init):
    lp = lam_ref[...]
    a = jnp.sum(lp[0:1, :] * lp[1:2, :], axis=1, keepdims=True)
    b = jnp.sum(lp[2:3, :] * lp[3:4, :], axis=1, keepdims=True)
    return jnp.exp(a) - jnp.exp(b) + lambda_init


def _pair_attn_t_kernel(*refs, nqb, n_prompt_blocks, fox, lambda_init):
    if fox:
        q_ref, k_ref, v_ref, km_ref, vm_ref, c3_ref, c3m_ref, cq_ref, o_ref, qbd_ref, vt_ref, acc_ref = refs
    else:
        q_ref, k_ref, v_ref, km_ref, vm_ref, lam_ref, sub_ref, o_ref, qbd_ref, vt_ref, acc_ref = refs
    qq = pl.program_id(0)
    is_meta = qq == pl.num_programs(0) - 1
    n_full, n_all = _frame_blocks(qq, nqb, n_prompt_blocks)
    cols = 2 * N_PAIRS * TQ
    vrows = HEAD_DIM if fox else DIFF_VDIM

    @pl.when(jnp.logical_and(qq < n_prompt_blocks, qq % nqb == 0))
    def _():
        @pl.loop(0, nqb)
        def _(kb):
            off = pl.multiple_of(kb * TQ, TQ)
            vt_ref[:, pl.ds(off, TQ)] = _transpose_bf16(v_ref[pl.ds(off, TQ), :])

    r2, c2 = _block_iotas(LANES, 2 * TQ)
    first = (r2 < HEAD_DIM) == (c2 < TQ)
    for g in range(N_PAIRS):
        qt = jnp.transpose(q_ref[:, g * LANES:(g + 1) * LANES].astype(F32))
        qt2 = jnp.concatenate([qt, qt], axis=1)
        qbd_ref[g, 0:LANES, :] = jnp.where(first, qt2, 0.0).astype(BF16)
        if fox:
            member = jnp.where(c2 < TQ, 0, 1)
            hit = jnp.logical_and(r2 >= BIAS_LANES * member, r2 < BIAS_LANES * (member + 1))
            qbd_ref[g, LANES:2 * LANES, :] = jnp.where(hit, -1.0, 0.0).astype(BF16)
    acc_ref[...] = jnp.zeros(acc_ref.shape, F32)
    rowc = cq_ref[0] if fox else None

    def update(carry, kblk, c3blk, vt, mask):
        m, l = carry
        parts = []
        for g in range(N_PAIRS):
            kg = kblk[:, g * LANES:(g + 1) * LANES]
            if fox:
                kg = jnp.concatenate([kg, c3blk[:, g * LANES:(g + 1) * LANES]], axis=1)
                parts.append(_dot(kg, qbd_ref[g]))
            else:
                parts.append(_dot(kg, qbd_ref[g, 0:LANES, :]))
        s = jnp.concatenate(parts, axis=1)
        if mask is not None:
            s = jnp.where(mask, s, NEG)
        smax = jnp.max(s, axis=0, keepdims=True)
        if fox:
            smax = smax + rowc
        m_new = jnp.maximum(m, smax)
        shift = m_new - rowc if fox else m_new
        alpha = jnp.exp(m - m_new)
        p = jnp.exp(s - shift)
        l = alpha * l + jnp.sum(p, axis=0, keepdims=True)
        p = p.astype(BF16)
        pvs = []
        for j in range(2 * N_PAIRS):
            vsl = slice(j * vrows, (j + 1) * vrows) if fox else slice((j // 2) * vrows, (j // 2 + 1) * vrows)
            pvs.append(_dot(vt[vsl, :], p[:, j * TQ:(j + 1) * TQ]))
        acc_ref[...] = acc_ref[...] * alpha + jnp.concatenate(pvs, axis=1)
        return m_new, l

    def step(kb, carry, masked):
        off = pl.multiple_of(kb * TQ, TQ)
        mask = None
        if masked:
            kk, qi = _block_iotas(TQ, cols)
            qi = qi % TQ
            mask = (kk <= qi) if fox else ((kk // CHUNK) <= (qi // CHUNK))
        c3blk = c3_ref[pl.ds(off, TQ), :] if fox else None
        return update(carry, k_ref[pl.ds(off, TQ), :], c3blk, vt_ref[:, pl.ds(off, TQ)], mask)

    kk, qi = _block_iotas(LANES, cols)
    mask = kk < N_META
    if fox:
        mask = jnp.logical_and(mask, jnp.logical_or(jnp.logical_not(is_meta), kk <= qi % TQ))
    carry = (jnp.full((1, cols), NEG, F32), jnp.zeros((1, cols), F32))
    carry = update(carry, km_ref[...], c3m_ref[...] if fox else None, _transpose_bf16(vm_ref[...]), mask)
    carry = lax.fori_loop(0, n_full, lambda kb, c: step(kb, c, False), carry)
    carry = lax.fori_loop(n_full, n_all, lambda kb, c: step(kb, c, True), carry)
    o = acc_ref[...] * (1.0 / carry[1])
    if fox:
        for g in range(N_PAIRS):
            pair = jnp.concatenate([o[:, (2 * g) * TQ:(2 * g + 1) * TQ], o[:, (2 * g + 1) * TQ:(2 * g + 2) * TQ]], axis=0)
            o_ref[:, g * LANES:(g + 1) * LANES] = jnp.transpose(pair).astype(o_ref.dtype)
    else:
        lam = _diff_lambda(lam_ref, lambda_init)
        for g in range(N_PAIRS):
            d = o[:, (2 * g) * TQ:(2 * g + 1) * TQ] - lam * o[:, (2 * g + 1) * TQ:(2 * g + 2) * TQ]
            ms = jnp.mean(d * d, axis=0, keepdims=True)
            d = (d * lax.rsqrt(ms + RMS_EPS)) * sub_ref[...] * (1.0 - lambda_init)
            o_ref[:, g * LANES:(g + 1) * LANES] = jnp.transpose(d).astype(o_ref.dtype)


def _pair_attn_t(q, k, v, extra, geo, fox, lambda_init=0.0):
    T = q.shape[0]
    nqb = geo.lp // TQ
    npb = geo.bp * nqb
    width = q.shape[1]
    once = pl.Buffered(1)
    bmap = lambda qq: (_prompt_batch(qq, nqb, geo.bp), 0)
    mmap = lambda qq: (geo.meta0 // LANES, 0)
    in_specs = [pl.BlockSpec((TQ, width), lambda qq: (qq, 0)),
                pl.BlockSpec((geo.lp, width), bmap, pipeline_mode=once),
                pl.BlockSpec((geo.lp, width), bmap, pipeline_mode=once),
                pl.BlockSpec((LANES, width), mmap),
                pl.BlockSpec((LANES, width), mmap)]
    if fox:
        c3, cq_row = extra
        in_specs += [pl.BlockSpec((geo.lp, width), bmap, pipeline_mode=once),
                     pl.BlockSpec((LANES, width), mmap),
                     pl.BlockSpec((1, 1, 2 * N_PAIRS * TQ), lambda qq: (qq, 0, 0))]
        args = (q, k, v, k, v, c3, c3, cq_row)
        vrows = HEAD_DIM
    else:
        lam_p, sub_col = extra
        in_specs += [pl.BlockSpec(lam_p.shape, lambda qq: (0, 0)), pl.BlockSpec(sub_col.shape, lambda qq: (0, 0))]
        args = (q, k, v, k, v, lam_p, sub_col)
        vrows = DIFF_VDIM
    kern = functools.partial(_pair_attn_t_kernel, nqb=nqb, n_prompt_blocks=npb, fox=fox, lambda_init=lambda_init)
    return pl.pallas_call(
        kern,
        grid=(T // TQ,),
        in_specs=in_specs,
        out_specs=pl.BlockSpec((TQ, width), lambda qq: (qq, 0)),
        out_shape=jax.ShapeDtypeStruct((T, width), BF16),
        scratch_shapes=[pltpu.VMEM((N_PAIRS, 2 * LANES, 2 * TQ), BF16),
                        pltpu.VMEM((width, geo.lp), BF16),
                        pltpu.VMEM((vrows, 2 * N_PAIRS * TQ), F32)],
        compiler_params=_cparams(("arbitrary",)),
        name="fox_prompt" if fox else "diff_prompt",
    )(*args)


def _mla_prompt_t_kernel(q_ref, k_ref, km_ref, wuvt_ref, o_ref, qt_ref, vt_ref, acc_ref, *, nqb, n_prompt_blocks):
    qq = pl.program_id(0)
    n_full, n_all = _frame_blocks(qq, nqb, n_prompt_blocks)

    @pl.when(jnp.logical_and(qq < n_prompt_blocks, qq % nqb == 0))
    def _():
        @pl.loop(0, nqb)
        def _(kb):
            off = pl.multiple_of(kb * TQ, TQ)
            vt_ref[:, pl.ds(off, TQ)] = _transpose_bf16(k_ref[pl.ds(off, TQ), 0:MLA_KV_RANK])

    cols = MLA_HEADS * TQ
    for hd in range(MLA_HEADS):
        qt_ref[:, hd * TQ:(hd + 1) * TQ] = _transpose_bf16(q_ref[:, hd * MLA_QW:(hd + 1) * MLA_QW])
    acc_ref[...] = jnp.zeros(acc_ref.shape, F32)

    def update(carry, kblk, vt, mask):
        m, l = carry
        s = _dot(kblk, qt_ref[...])
        if mask is not None:
            s = jnp.where(mask, s, NEG)
        m_new = jnp.maximum(m, jnp.max(s, axis=0, keepdims=True))
        alpha = jnp.exp(m - m_new)
        p = jnp.exp(s - m_new)
        l = alpha * l + jnp.sum(p, axis=0, keepdims=True)
        acc_ref[...] = acc_ref[...] * alpha + _dot(vt, p.astype(BF16))
        return m_new, l

    def step(kb, carry, masked):
        off = pl.multiple_of(kb * TQ, TQ)
        mask = None
        if masked:
            kk, qi = _block_iotas(TQ, cols)
            mask = (kk // CHUNK) <= ((qi % TQ) // CHUNK)
        return update(carry, k_ref[pl.ds(off, TQ), :], vt_ref[:, pl.ds(off, TQ)], mask)

    km = km_ref[...]
    kk, _ = _block_iotas(LANES, cols)
    carry = (jnp.full((1, cols), NEG, F32), jnp.zeros((1, cols), F32))
    carry = update(carry, km, _transpose_bf16(km[:, 0:MLA_KV_RANK]), kk < N_META)
    carry = lax.fori_loop(0, n_full, lambda kb, c: step(kb, c, False), carry)
    carry = lax.fori_loop(n_full, n_all, lambda kb, c: step(kb, c, True), carry)
    inv_l = 1.0 / carry[1]
    for hd in range(MLA_HEADS):
        sl = slice(hd * TQ, (hd + 1) * TQ)
        olat = (acc_ref[:, sl] * inv_l[:, sl]).astype(BF16)
        o = jnp.transpose(_dot(wuvt_ref[hd], olat))
        o_ref[:, hd * MLA_VDIM:(hd + 1) * MLA_VDIM] = o.astype(o_ref.dtype)


def _mla_prompt_t(qcat, kcat, wuvt, geo):
    T = qcat.shape[0]
    nqb = geo.lp // TQ
    npb = geo.bp * nqb
    kern = functools.partial(_mla_prompt_t_kernel, nqb=nqb, n_prompt_blocks=npb)
    return pl.pallas_call(
        kern,
        grid=(T // TQ,),
        in_specs=[pl.BlockSpec((TQ, MLA_HEADS * MLA_QW), lambda qq: (qq, 0)),
                  pl.BlockSpec((geo.lp, MLA_QW), lambda qq: (_prompt_batch(qq, nqb, geo.bp), 0)),
                  pl.BlockSpec((LANES, MLA_QW), lambda qq: (geo.meta0 // LANES, 0)),
                  pl.BlockSpec(wuvt.shape, lambda qq: (0, 0, 0))],
        out_specs=pl.BlockSpec((TQ, MLA_HEADS * MLA_VDIM), lambda qq: (qq, 0)),
        out_shape=jax.ShapeDtypeStruct((T, MLA_HEADS * MLA_VDIM), BF16),
        scratch_shapes=[pltpu.VMEM((MLA_QW, MLA_HEADS * TQ), BF16),
                        pltpu.VMEM((MLA_KV_RANK, geo.lp), BF16),
                        pltpu.VMEM((MLA_KV_RANK, MLA_HEADS * TQ), F32)],
        compiler_params=_cparams(("arbitrary",)),
        name="mla_prompt",
    )(qcat, kcat, kcat, wuvt)


def _block_diag_queries(q, n_blocks, width):
    lane = lax.broadcasted_iota(jnp.int32, q.shape, 1)
    zero = jnp.zeros_like(q)
    return jnp.concatenate([jnp.where(lane // width == r, q, zero) for r in range(n_blocks)], axis=0)


def _expand_rows(c, ls):
    return jnp.concatenate([jnp.broadcast_to(c[r:r + 1, :], (ls, c.shape[1])) for r in range(c.shape[0])], axis=0)


def _three_part_softmax(parts):
    m = parts[0][0].max(axis=1, keepdims=True)
    for s, _ in parts[1:]:
        m = jnp.maximum(m, s.max(axis=1, keepdims=True))
    l = 0.0
    acc = 0.0
    for s, v in parts:
        p = jnp.exp(s - m)
        l = l + jnp.sum(p, axis=1, keepdims=True)
        acc = acc + _dot(p.astype(BF16), v)
    return acc / l


def _fox_sample_kernel(q_ref, kn_ref, vn_ref, km_ref, vm_ref, kp_ref, vp_ref,
                       cq_ref, ckm_ref, ckp_ref, ckn_ref, prev_ref, o_ref):
    del prev_ref
    ls = q_ref.shape[0]
    qbd = _block_diag_queries(q_ref[...], FOX_HEADS, HEAD_DIM)
    cq = cq_ref[...]
    rowc = jnp.concatenate([cq[:, r:r + 1] for r in range(FOX_HEADS)], axis=0)
    s_meta = _nt_dot(qbd, km_ref[...]) - _expand_rows(ckm_ref[...], ls) + rowc
    s_past = _nt_dot(qbd, kp_ref[0].astype(BF16)) - _expand_rows(ckp_ref[0], ls) + rowc
    s_new = _nt_dot(qbd, kn_ref[...]) - _expand_rows(ckn_ref[0], ls) + rowc
    rr, cc = _block_iotas(FOX_HEADS * ls, ls)
    s_new = jnp.where(cc <= rr % ls, s_new, NEG)
    obd = _three_part_softmax([(s_meta, vm_ref[...]), (s_past, vp_ref[0].astype(BF16)), (s_new, vn_ref[...])])
    lane = lax.broadcasted_iota(jnp.int32, (ls, FOX_W), 1)
    out = jnp.zeros((ls, FOX_W), F32)
    for r in range(FOX_HEADS):
        out = out + jnp.where(lane // HEAD_DIM == r, obd[r * ls:(r + 1) * ls, :], 0.0)
    o_ref[...] = out.astype(o_ref.dtype)


def _fox_sample(qf, kf, vf, past_k, past_v, c_col8, c_row_meta8, c_row_past, c_row_new, prev, geo):
    ls, bs, P = geo.ls, geo.bs, geo.past
    row0 = geo.sample0 // ls
    rmap = lambda s: (row0 + s, 0)
    mmap = lambda s: (geo.meta0 // N_META, 0)
    return pl.pallas_call(
        _fox_sample_kernel,
        grid=(bs,),
        in_specs=[pl.BlockSpec((ls, FOX_W), rmap), pl.BlockSpec((ls, FOX_W), rmap), pl.BlockSpec((ls, FOX_W), rmap),
                  pl.BlockSpec((N_META, FOX_W), mmap), pl.BlockSpec((N_META, FOX_W), mmap),
                  pl.BlockSpec((1, P, FOX_W), lambda s: (s, 0, 0)), pl.BlockSpec((1, P, FOX_W), lambda s: (s, 0, 0)),
                  pl.BlockSpec((ls, FOX_HEADS), rmap),
                  pl.BlockSpec((FOX_HEADS, N_META), lambda s: (0, 0)),
                  pl.BlockSpec((1, FOX_HEADS, P), lambda s: (s, 0, 0)),
                  pl.BlockSpec((1, FOX_HEADS, ls), lambda s: (s, 0, 0)),
                  pl.BlockSpec(memory_space=pl.ANY)],
        out_specs=pl.BlockSpec((ls, FOX_W), rmap),
        out_shape=jax.ShapeDtypeStruct(prev.shape, prev.dtype),
        input_output_aliases={11: 0},
        compiler_params=_cparams(("parallel",)),
        name="fox_sample",
    )(qf, kf, vf, kf, vf, past_k, past_v, c_col8, c_row_meta8, c_row_past, c_row_new, prev)


def _diff_sample_kernel(q_ref, kn_ref, vn_ref, km_ref, vm_ref, kp_ref, vp_ref, lam_ref, sub_ref, prev_ref, o_ref, *,
                        lambda_init, past):
    del prev_ref
    ls = q_ref.shape[0]
    nb = 2 * DIFF_HEADS
    qbd = _block_diag_queries(q_ref[...], nb, HEAD_DIM)
    s_meta = _nt_dot(qbd, km_ref[...])
    s_past = _nt_dot(qbd, kp_ref[0].astype(BF16))
    s_new = _nt_dot(qbd, kn_ref[...])
    rr, cc = _block_iotas(nb * ls, ls)
    s_new = jnp.where((past + cc) // CHUNK <= (past + rr % ls) // CHUNK, s_new, NEG)
    obd = _three_part_softmax([(s_meta, vm_ref[...]), (s_past, vp_ref[0].astype(BF16)), (s_new, vn_ref[...])])
    width = DIFF_HEADS * DIFF_VDIM
    lane = lax.broadcasted_iota(jnp.int32, (ls, width), 1)
    o0 = jnp.zeros((ls, width), F32)
    o1 = jnp.zeros((ls, width), F32)
    for hd in range(DIFF_HEADS):
        sel = lane // DIFF_VDIM == hd
        o0 = o0 + jnp.where(sel, obd[(2 * hd) * ls:(2 * hd + 1) * ls, :], 0.0)
        o1 = o1 + jnp.where(sel, obd[(2 * hd + 1) * ls:(2 * hd + 2) * ls, :], 0.0)
    o = o0 - _diff_lambda(lam_ref, lambda_init) * o1
    sub = sub_ref[...]
    segs = []
    for hd in range(DIFF_HEADS):
        segs.append(_rms(o[:, hd * DIFF_VDIM:(hd + 1) * DIFF_VDIM], sub) * (1.0 - lambda_init))
    o_ref[...] = jnp.concatenate(segs, axis=1).astype(o_ref.dtype)


def _diff_sample(qd, kd, vd, past_k, past_v, lam_p, subln, prev, geo, lambda_init):
    ls, bs, P = geo.ls, geo.bs, geo.past
    width = DIFF_HEADS * DIFF_VDIM
    row0 = geo.sample0 // ls
    rmap = lambda s: (row0 + s, 0)
    mmap = lambda s: (geo.meta0 // N_META, 0)
    full = lambda a: pl.BlockSpec(a.shape, lambda s: (0,) * a.ndim)
    kern = functools.partial(_diff_sample_kernel, lambda_init=lambda_init, past=P)
    return pl.pallas_call(
        kern,
        grid=(bs,),
        in_specs=[pl.BlockSpec((ls, width), rmap), pl.BlockSpec((ls, width), rmap), pl.BlockSpec((ls, width), rmap),
                  pl.BlockSpec((N_META, width), mmap), pl.BlockSpec((N_META, width), mmap),
                  pl.BlockSpec((1, P, width), lambda s: (s, 0, 0)), pl.BlockSpec((1, P, width), lambda s: (s, 0, 0)),
                  full(lam_p), full(subln),
                  pl.BlockSpec(memory_space=pl.ANY)],
        out_specs=pl.BlockSpec((ls, width), rmap),
        out_shape=jax.ShapeDtypeStruct(prev.shape, prev.dtype),
        input_output_aliases={9: 0},
        compiler_params=_cparams(("parallel",)),
        name="diff_sample",
    )(qd, kd, vd, kd, vd, past_k, past_v, lam_p, subln, prev)


def _mla_sample_kernel(q_ref, kn_ref, km_ref, cp_ref, rp_ref, wuv_ref, prev_ref, o_ref, *, past):
    del prev_ref
    ls = q_ref.shape[0]
    q = q_ref[...]
    qst = jnp.concatenate([q[:, hd * MLA_QW:(hd + 1) * MLA_QW] for hd in range(MLA_HEADS)], axis=0)
    ckv = cp_ref[0].astype(BF16)
    kro = rp_ref[0].astype(BF16)
    kn = kn_ref[...]
    km = km_ref[...]
    s_meta = _nt_dot(qst, km)
    s_past = (_nt_dot(qst[:, 0:MLA_KV_RANK], ckv)
              + _nt_dot(qst[:, MLA_KV_RANK:MLA_KV_RANK + MLA_ROPE], kro))
    s_new = _nt_dot(qst, kn)
    rr, cc = _block_iotas(MLA_HEADS * ls, ls)
    s_new = jnp.where((past + cc) // CHUNK <= (past + rr % ls) // CHUNK, s_new, NEG)
    olat = _three_part_softmax([(s_meta, km[:, 0:MLA_KV_RANK]), (s_past, ckv), (s_new, kn[:, 0:MLA_KV_RANK])])
    outs = []
    for hd in range(MLA_HEADS):
        outs.append(_dot(olat[hd * ls:(hd + 1) * ls, :].astype(BF16), wuv_ref[hd]))
    o_ref[...] = jnp.concatenate(outs, axis=1).astype(o_ref.dtype)


def _mla_sample(qcat, kcat, past_ckv, past_kr, wuv, prev, geo):
    ls, bs, P = geo.ls, geo.bs, geo.past
    row0 = geo.sample0 // ls
    rmap = lambda s: (row0 + s, 0)
    kern = functools.partial(_mla_sample_kernel, past=P)
    return pl.pallas_call(
        kern,
        grid=(bs,),
        in_specs=[pl.BlockSpec((ls, MLA_HEADS * MLA_QW), rmap),
                  pl.BlockSpec((ls, MLA_QW), rmap),
                  pl.BlockSpec((N_META, MLA_QW), lambda s: (geo.meta0 // N_META, 0)),
                  pl.BlockSpec((1, P, MLA_KV_RANK), lambda s: (s, 0, 0)),
                  pl.BlockSpec((1, P, MLA_ROPE), lambda s: (s, 0, 0)),
                  pl.BlockSpec(wuv.shape, lambda s: (0, 0, 0)),
                  pl.BlockSpec(memory_space=pl.ANY)],
        out_specs=pl.BlockSpec((ls, MLA_HEADS * MLA_VDIM), rmap),
        out_shape=jax.ShapeDtypeStruct(prev.shape, prev.dtype),
        input_output_aliases={6: 0},
        compiler_params=_cparams(("parallel",)),
        name="mla_sample",
    )(qcat, kcat, kcat, past_ckv, past_kr, wuv, prev)


def _post_attn_kernel(*refs, n_in):
    a_refs = refs[:n_in]
    w_refs = refs[n_in:2 * n_in]
    x_ref, g_ref, wr_ref, br_ref = refs[2 * n_in:2 * n_in + 4]
    x1_ref, xs_ref, gs_ref, lp_ref, cnt_ref = refs[2 * n_in + 4:]

    y = _dot(a_refs[0][...], w_refs[0][...])
    for a_ref, w_ref in zip(a_refs[1:], w_refs[1:]):
        y = y + _dot(a_ref[...], w_ref[...])
    x1 = x_ref[...] + y
    x1_ref[...] = x1
    h = _rms(x1, g_ref[...])

    logits = jnp.dot(h, wr_ref[...], precision=HIGHEST, preferred_element_type=F32) + br_ref[...]
    lane = lax.broadcasted_iota(jnp.int32, (TM, LANES), 1)
    lanef = lane.astype(F32)
    is_g = jnp.logical_and(lane >= N_EXPERTS, lane < N_EXPERTS + N_GROUPS)
    gl = jnp.where(is_g, logits, -jnp.inf)
    gmax = jnp.max(gl, axis=1, keepdims=True)
    gsel = jnp.min(jnp.where(gl == gmax, lanef - N_EXPERTS, 1e9), axis=1, keepdims=True)
    p_g = 1.0 / jnp.sum(jnp.exp(gl - gmax), axis=1, keepdims=True)
    lane_group = (lane // EXPERTS_PER_GROUP).astype(F32)
    in_group = jnp.logical_and(lane < N_EXPERTS, lane_group == gsel)
    el = jnp.where(in_group, logits, -jnp.inf)
    v1 = jnp.max(el, axis=1, keepdims=True)
    i1 = jnp.min(jnp.where(el == v1, lanef, 1e9), axis=1, keepdims=True)
    el2 = jnp.where(lanef == i1, -jnp.inf, el)
    v2 = jnp.max(el2, axis=1, keepdims=True)
    i2 = jnp.min(jnp.where(el2 == v2, lanef, 1e9), axis=1, keepdims=True)
    e2 = jnp.exp(v2 - v1)
    den = 1.0 + e2
    wa = (1.0 / den) * p_g
    wb = (e2 / den) * p_g
    j1 = i1 - EXPERTS_PER_GROUP * gsel
    j2 = i2 - EXPERTS_PER_GROUP * gsel
    gate = jnp.where(lanef == j1, wa, 0.0) + jnp.where(lanef == j2, wb, 0.0)

    ohg = jnp.where(lanef == gsel, 1.0, 0.0)
    counts = jnp.sum(ohg, axis=0, keepdims=True)
    rows16 = jnp.floor((counts + (ROW_CHUNK - 1)) * (1.0 / ROW_CHUNK)) * ROW_CHUNK
    r128, c128 = _block_iotas(LANES, LANES)
    before = jnp.where(r128 < c128, 1.0, 0.0).astype(BF16)
    off = _dot(jnp.broadcast_to(rows16, (8, LANES)).astype(BF16), before)[0:1, :]
    rt, ct = _block_iotas(TM, TM)
    earlier = jnp.where(ct < rt, 1.0, 0.0).astype(BF16)
    rank = _dot(earlier, ohg.astype(BF16))
    lp = jnp.sum(ohg * (off + rank), axis=1, keepdims=True)
    lp_b = jnp.broadcast_to(lp, (TM, LANES))
    lp_ref[...] = lp_b
    cnt_ref[0] = jnp.broadcast_to(counts, (8, LANES))
    lp_row = jnp.transpose(lp_b)[0:1, :]
    rloc = lax.broadcasted_iota(jnp.int32, (LOCAL_ROWS, TM), 0).astype(F32)
    perm = jnp.where(rloc == lp_row, 1.0, 0.0)
    xs_ref[...] = _dot(perm.astype(BF16), h.astype(BF16)).astype(BF16)
    gs_ref[...] = jnp.dot(perm, gate, precision=HIGHEST, preferred_element_type=F32)


def _post_attn(attn_list, w_list, x, g, wr, br):
    T = x.shape[0]
    nt = T // TM
    row = lambda w: pl.BlockSpec((TM, w), lambda i: (i, 0))
    full = lambda a: pl.BlockSpec(a.shape, lambda i: (0,) * a.ndim)
    kern = functools.partial(_post_attn_kernel, n_in=len(attn_list))
    outs = [jax.ShapeDtypeStruct((T, x.shape[1]), F32),
            jax.ShapeDtypeStruct((nt * LOCAL_ROWS, x.shape[1]), BF16),
            jax.ShapeDtypeStruct((nt * LOCAL_ROWS, LANES), F32),
            jax.ShapeDtypeStruct((T, LANES), F32),
            jax.ShapeDtypeStruct((nt, 8, LANES), F32)]
    out_specs = [row(x.shape[1]),
                 pl.BlockSpec((LOCAL_ROWS, x.shape[1]), lambda i: (i, 0)),
                 pl.BlockSpec((LOCAL_ROWS, LANES), lambda i: (i, 0)),
                 row(LANES),
                 pl.BlockSpec((1, 8, LANES), lambda i: (i, 0, 0))]
    return pl.pallas_call(
        kern,
        grid=(nt,),
        in_specs=([row(a.shape[1]) for a in attn_list] + [full(w) for w in w_list]
                  + [row(x.shape[1]), full(g), full(wr), full(br)]),
        out_specs=out_specs,
        out_shape=outs,
        compiler_params=_cparams(("parallel",)),
        name="post_attn_route",
    )(*attn_list, *w_list, x, g, wr, br)


def _chunk_tables(cnt, n_tiles):
    nt = cnt.shape[0]
    chunks = jnp.ceil(cnt / ROW_CHUNK).astype(jnp.int32)
    cum = jnp.cumsum(chunks, axis=1)
    c = jnp.arange(LOCAL_CHUNKS, dtype=jnp.int32)
    keys = jnp.sum(c[None, :, None] >= cum[:, None, :], axis=2).reshape(-1)
    order = jnp.argsort(keys, stable=True).astype(jnp.int32)
    ng = N_GROUPS + 1
    n_g = jnp.sum(keys[:, None] == jnp.arange(ng)[None, :], axis=0).astype(jnp.int32)
    tiles_g = (n_g + EXPERT_TILE_CHUNKS - 1) // EXPERT_TILE_CHUNKS
    tile_end = jnp.cumsum(tiles_g)
    start_slot = (tile_end - tiles_g) * EXPERT_TILE_CHUNKS
    start_sorted = jnp.cumsum(n_g) - n_g
    gj = keys[order]
    j = jnp.arange(nt * LOCAL_CHUNKS, dtype=jnp.int32)
    slot = start_slot[gj] + (j - start_sorted[gj])
    src = jnp.full((n_tiles * EXPERT_TILE_CHUNKS,), -1, jnp.int32).at[slot].set(order)
    tgrp = jnp.sum(jnp.arange(n_tiles)[:, None] >= tile_end[None, :], axis=1)
    tgrp = jnp.minimum(tgrp, N_GROUPS).astype(jnp.int32)
    return src, tgrp


def _experts_kernel(src_ref, tgrp_ref, xs_hbm, gs_hbm, wg_ref, wu_ref, wd_ref, ys_hbm,
                    wgb, wub, wdb, xbuf, gbuf, obuf, sem_x, sem_g, sem_o):
    i = pl.program_id(0)
    n = pl.num_programs(0)
    slot = i % 2

    def chunk_rows(cid):
        return pl.ds(pl.multiple_of(cid * ROW_CHUNK, ROW_CHUNK), ROW_CHUNK)

    def in_copies(tile, s, c):
        sid = src_ref[tile * EXPERT_TILE_CHUNKS + c]
        rid = jnp.where(sid < 0, LOCAL_CHUNKS - 1, sid)
        dst = pl.ds(c * ROW_CHUNK, ROW_CHUNK)
        return (pltpu.make_async_copy(xs_hbm.at[chunk_rows(rid), :], xbuf.at[s, dst, :], sem_x.at[s]),
                pltpu.make_async_copy(gs_hbm.at[chunk_rows(rid), :], gbuf.at[s, dst, :], sem_g.at[s]))

    def gather(tile, s, wait):
        for c in range(EXPERT_TILE_CHUNKS):
            for cp in in_copies(tile, s, c):
                cp.wait() if wait else cp.start()

    def scatter(tile, s, wait):
        for c in range(EXPERT_TILE_CHUNKS):
            sid = src_ref[tile * EXPERT_TILE_CHUNKS + c]
            cp = pltpu.make_async_copy(obuf.at[s, pl.ds(c * ROW_CHUNK, ROW_CHUNK), :],
                                       ys_hbm.at[chunk_rows(jnp.maximum(sid, 0)), :], sem_o.at[s])

            @pl.when(sid >= 0)
            def _():
                cp.wait() if wait else cp.start()

    @pl.when(i == 0)
    def _():
        gather(0, 0, False)

    @pl.when(i + 1 < n)
    def _():
        gather(i + 1, 1 - slot, False)

    gather(i, slot, True)

    @pl.when(i >= 2)
    def _():
        scatter(i - 2, slot, True)

    grp = tgrp_ref[i]
    prev = tgrp_ref[jnp.maximum(i - 1, 0)]
    is_filler = grp >= N_GROUPS

    @pl.when(jnp.logical_and(jnp.logical_not(is_filler), jnp.logical_or(i == 0, grp != prev)))
    def _():
        wgb[...] = wg_ref[...].astype(BF16)
        wub[...] = wu_ref[...].astype(BF16)
        wdb[...] = wd_ref[...].astype(BF16)

    @pl.when(is_filler)
    def _():
        obuf[slot] = jnp.zeros(obuf.shape[1:], F32)

    @pl.when(jnp.logical_not(is_filler))
    def _():
        x = xbuf[slot]
        gate = gbuf[slot]
        parts = []
        for e in range(EXPERTS_PER_GROUP):
            a = _dot(x, wgb[e])
            b = _dot(x, wub[e])
            parts.append((a * jax.nn.sigmoid(a) * b * gate[:, e:e + 1]).astype(BF16))
        obuf[slot] = _dot(jnp.concatenate(parts, axis=1), wdb[...])

    scatter(i, slot, False)

    @pl.when(i == n - 1)
    def _():
        scatter(i, slot, True)

        @pl.when(i >= 1)
        def _():
            scatter(i - 1, 1 - slot, True)


def _experts(xs, gs, src, tgrp, wg, wu, wd):
    n_tiles = tgrp.shape[0]
    d = xs.shape[1]
    gw = EXPERTS_PER_GROUP * EXPERT_FF
    once = pl.Buffered(1)
    wmap = lambda i, src, tgrp: (jnp.minimum(tgrp[i], N_GROUPS - 1), 0, 0, 0)
    grid_spec = pltpu.PrefetchScalarGridSpec(
        num_scalar_prefetch=2,
        grid=(n_tiles,),
        in_specs=[pl.BlockSpec(memory_space=pl.ANY),
                  pl.BlockSpec(memory_space=pl.ANY),
                  pl.BlockSpec((None, EXPERTS_PER_GROUP, d, EXPERT_FF), wmap, pipeline_mode=once),
                  pl.BlockSpec((None, EXPERTS_PER_GROUP, d, EXPERT_FF), wmap, pipeline_mode=once),
                  pl.BlockSpec((None, gw, d), lambda i, src, tgrp: (jnp.minimum(tgrp[i], N_GROUPS - 1), 0, 0),
                               pipeline_mode=once)],
        out_specs=pl.BlockSpec(memory_space=pl.ANY),
        scratch_shapes=[pltpu.VMEM((EXPERTS_PER_GROUP, d, EXPERT_FF), BF16),
                        pltpu.VMEM((EXPERTS_PER_GROUP, d, EXPERT_FF), BF16),
                        pltpu.VMEM((gw, d), BF16),
                        pltpu.VMEM((2, TM, d), BF16), pltpu.VMEM((2, TM, LANES), F32), pltpu.VMEM((2, TM, d), F32),
                        pltpu.SemaphoreType.DMA((2,)), pltpu.SemaphoreType.DMA((2,)), pltpu.SemaphoreType.DMA((2,))],
    )
    return pl.pallas_call(
        _experts_kernel,
        grid_spec=grid_spec,
        out_shape=jax.ShapeDtypeStruct((xs.shape[0], d), F32),
        compiler_params=_cparams(("arbitrary",)),
        name="moe_experts",
    )(src, tgrp, xs, gs, wg, wu, wd)


def _unpermute(ys_ref, lp_ref, d):
    yl = jnp.concatenate([ys_ref[...], jnp.zeros((LOCAL_ROWS_PAD - LOCAL_ROWS, d), F32)], axis=0)
    lp = lp_ref[...][:, 0:1]
    lanef = lax.broadcasted_iota(jnp.int32, (TM, LOCAL_ROWS_PAD), 1).astype(F32)
    sel = jnp.where(lanef == lp, 1.0, 0.0).astype(BF16)
    hi = yl.astype(BF16)
    r1 = yl - hi.astype(F32)
    mid = r1.astype(BF16)
    lo = (r1 - mid.astype(F32)).astype(BF16)
    return (_dot(sel, hi) + _dot(sel, mid)) + _dot(sel, lo)


def _moe_combine_kernel(x1_ref, ys_ref, lp_ref, o_ref):
    o_ref[...] = x1_ref[...] + _unpermute(ys_ref, lp_ref, x1_ref.shape[1])


def _moe_combine_final_kernel(x1_ref, ys_ref, lp_ref, gfin_ref, yp_ref, ysm_ref, *, npt, nst):
    i = pl.program_id(0)
    y = _rms(x1_ref[...] + _unpermute(ys_ref, lp_ref, x1_ref.shape[1]), gfin_ref[...])

    @pl.when(i < npt)
    def _():
        yp_ref[...] = y

    @pl.when(jnp.logical_and(i >= npt, i < npt + nst))
    def _():
        ysm_ref[...] = y


def _moe_combine(x1, ys, lp, gfin, geo, final):
    T, d = x1.shape
    row = lambda w: pl.BlockSpec((TM, w), lambda i: (i, 0))
    in_specs = [row(d), pl.BlockSpec((LOCAL_ROWS, d), lambda i: (i, 0)), row(LANES)]
    if not final:
        return pl.pallas_call(
            _moe_combine_kernel,
            grid=(T // TM,),
            in_specs=in_specs,
            out_specs=row(d),
            out_shape=jax.ShapeDtypeStruct((T, d), F32),
            compiler_params=_cparams(("parallel",)),
            name="moe_combine",
        )(x1, ys, lp)
    npt, nst = geo.prompt_rows // TM, geo.sample_rows // TM
    return pl.pallas_call(
        functools.partial(_moe_combine_final_kernel, npt=npt, nst=nst),
        grid=(T // TM,),
        in_specs=in_specs + [pl.BlockSpec(gfin.shape, lambda i: (0, 0))],
        out_specs=[pl.BlockSpec((TM, d), lambda i: (jnp.minimum(i, npt - 1), 0)),
                   pl.BlockSpec((TM, d), lambda i: (jnp.clip(i - npt, 0, nst - 1), 0))],
        out_shape=[jax.ShapeDtypeStruct((geo.prompt_rows, d), F32), jax.ShapeDtypeStruct((geo.sample_rows, d), F32)],
        compiler_params=_cparams(("arbitrary",)),
        name="moe_combine_final",
    )(x1, ys, lp, gfin)


class _Geometry:
    def __init__(self, bp, lp, bs, ls, past):
        self.bp, self.lp, self.bs, self.ls, self.past = bp, lp, bs, ls, past
        self.prompt_rows = bp * lp
        self.sample0 = self.prompt_rows
        self.sample_rows = bs * ls
        self.meta0 = self.prompt_rows + self.sample_rows
        self.total = self.meta0 + TM
        assert lp % TQ == 0 and self.sample_rows % TM == 0 and ls % ROW_CHUNK == 0


def _rotary_tables(geo):
    pos = np.zeros((geo.total,), np.int32)
    pos[:geo.prompt_rows] = np.tile(N_META + np.arange(geo.lp), geo.bp)
    pos[geo.sample0:geo.meta0] = np.tile(N_META + geo.past + np.arange(geo.ls), geo.bs)
    pos[geo.meta0:geo.meta0 + N_META] = np.arange(N_META)
    inv = ROPE_THETA ** (-jnp.arange(0, HEAD_DIM, 2, dtype=F32) / HEAD_DIM)
    ang = jnp.asarray(pos).astype(F32)[:, None] * inv[None, :]
    cos = jnp.concatenate([jnp.cos(ang), jnp.cos(ang)], -1)
    sin = jnp.concatenate([jnp.sin(ang), jnp.sin(ang)], -1)
    zeros = jnp.zeros_like(cos)
    return (jnp.concatenate([cos, cos], 1), jnp.concatenate([sin, sin], 1),
            jnp.concatenate([cos, zeros], 1), jnp.concatenate([sin, zeros], 1))


def _keep_bf16_bits(x):
    bits = lax.bitcast_convert_type(x, jnp.uint32) & jnp.uint32(0xFFFF0000)
    return lax.bitcast_convert_type(bits, F32)


def _rot_cols(w, group):
    shp = w.shape
    wg = w.reshape(shp[:-1] + (shp[-1] // group, 2, group // 2))
    return jnp.concatenate([-wg[..., 1:2, :], wg[..., 0:1, :]], axis=-2).reshape(shp)


def _moe_layer(attn_list, w_list, x, g_ffn, wr, br, w_gate, w_up, w_down, gfin, geo, final):
    d = x.shape[1]
    nt = x.shape[0] // TM
    n_tiles = -(-(nt * LOCAL_CHUNKS) // EXPERT_TILE_CHUNKS) + N_GROUPS + 1
    x1, xs, gs, lp, cnt = _post_attn(attn_list, w_list, x, g_ffn, wr, br)
    src, tgrp = _chunk_tables(cnt[:, 0, :N_GROUPS], n_tiles)
    ys = _experts(xs, gs, src, tgrp,
                  w_gate.reshape(N_GROUPS, EXPERTS_PER_GROUP, d, EXPERT_FF),
                  w_up.reshape(N_GROUPS, EXPERTS_PER_GROUP, d, EXPERT_FF),
                  w_down.reshape(N_GROUPS, EXPERTS_PER_GROUP * EXPERT_FF, d))
    return _moe_combine(x1, ys, lp, gfin, geo, final)


def _router_weights(w_group, b_group, w_router, b_router):
    d = w_group.shape[0]
    wr = jnp.zeros((d, LANES), F32).at[:, :N_EXPERTS].set(w_router).at[:, N_EXPERTS:N_EXPERTS + N_GROUPS].set(w_group)
    br = jnp.zeros((1, LANES), F32).at[0, :N_EXPERTS].set(b_router).at[0, N_EXPERTS:N_EXPERTS + N_GROUPS].set(b_group)
    return wr, br


def kernel(x_prompt, x_sample, cache_fox_k, cache_fox_v, cache_fox_logf, cache_diff_k, cache_diff_v, cache_mla_ckv, cache_mla_krope, meta_tokens, norm_mix, norm_ffn, norm_final, w_in_even, fox_b_f, diff_lambda, diff_subln, w_out_even, w_in_odd, mla_norm_q, mla_norm_kv, mla_w_uq, mla_w_uk, mla_w_uv, w_out_odd, moe_w_group, moe_b_group, moe_w_router, moe_b_router, moe_w_gate, moe_w_up, moe_w_down):
    bp, lp, d = x_prompt.shape
    bs, ls, _ = x_sample.shape
    past = cache_fox_k.shape[2]
    geo = _Geometry(bp, lp, bs, ls, past)
    T, PR, SR, M0 = geo.total, geo.prompt_rows, geo.sample_rows, geo.meta0
    L = N_META + lp
    cos2, sin2, cosp, sinp = _rotary_tables(geo)

    x = jnp.concatenate([x_prompt.reshape(PR, d), x_sample.reshape(SR, d), meta_tokens,
                         jnp.zeros((TM - N_META, d), F32)], axis=0)

    w = w_in_even[0]
    o = 3 * FOX_W
    wq_d = w[:, o + FOX_HEADS:o + FOX_HEADS + 512]
    wk_d = w[:, o + FOX_HEADS + 512:o + FOX_HEADS + 1024]
    wv_d = w[:, o + FOX_HEADS + 1024:o + FOX_HEADS + 1536]
    wf = jnp.zeros((d, LANES), F32).at[:, :FOX_HEADS].set(w[:, o:o + FOX_HEADS])
    w_all = jnp.concatenate([w[:, 0:o], wf, wq_d, _rot_cols(wq_d, HEAD_DIM), wk_d, _rot_cols(wk_d, HEAD_DIM), wv_d],
                            axis=1).astype(BF16)
    b_f = jnp.zeros((1, LANES), F32).at[0, :FOX_HEADS].set(fox_b_f[0])
    (qf, kf, vf, qd, kd, vd, lf, kfp, vfp, kdp, vdp, kfs, vfs, kds, vds) = _proj_even(
        x, norm_mix[0][None, :], w_all, b_f, cos2, sin2, geo)

    lf8 = lf[:, :FOX_HEADS]
    meta_lf = lf8[M0:M0 + N_META]
    rows_p = jnp.concatenate([jnp.broadcast_to(meta_lf[None], (bp, N_META, FOX_HEADS)),
                              lf8[:PR].reshape(bp, lp, FOX_HEADS)], axis=1)
    rows_s = jnp.concatenate([jnp.broadcast_to(meta_lf[None], (bs, N_META, FOX_HEADS)),
                              cache_fox_logf[0], lf8[PR:PR + SR].reshape(bs, ls, FOX_HEADS)], axis=1)
    c_p = _cumsum_rows(rows_p.transpose(0, 2, 1).reshape(bp * FOX_HEADS, L))
    c_s = _cumsum_rows(rows_s.transpose(0, 2, 1).reshape(bs * FOX_HEADS, N_META + past + ls))
    c_meta_row = c_p[:FOX_HEADS, :N_META]
    c_prompt = c_p[:, N_META:L].reshape(bp, FOX_HEADS, lp)
    c_past = c_s[:, N_META:N_META + past].reshape(bs, FOX_HEADS, past)
    c_new = c_s[:, N_META + past:N_META + past + ls].reshape(bs, FOX_HEADS, ls)
    c_col8 = jnp.concatenate([c_prompt.transpose(0, 2, 1).reshape(PR, FOX_HEADS),
                              c_new.transpose(0, 2, 1).reshape(SR, FOX_HEADS),
                              c_meta_row.T, jnp.zeros((TM - N_META, FOX_HEADS), F32)], axis=0)

    c_hi = _keep_bf16_bits(c_col8)
    c_mid = _keep_bf16_bits(c_col8 - c_hi)
    c_lo = (c_col8 - c_hi) - c_mid
    c3 = jnp.stack([c_hi, c_mid, c_lo], axis=-1).astype(BF16).reshape(T, N_PAIRS, 2 * BIAS_LANES)
    c3 = jnp.pad(c3, ((0, 0), (0, 0), (0, LANES - 2 * BIAS_LANES))).reshape(T, FOX_W)
    cq_row = c_col8.reshape(T // TQ, TQ, FOX_HEADS).transpose(0, 2, 1).reshape(T // TQ, 1, FOX_HEADS * TQ)

    lambda_init = 0.8 - 0.6 * math.exp(-0.3 * 0)
    lam_p = diff_lambda[0]
    fox_o = _pair_attn_t(qf, kf, vf, (c3, cq_row), geo, True)
    fox_o = _fox_sample(qf, kf, vf, cache_fox_k[0].reshape(bs, past, FOX_W), cache_fox_v[0].reshape(bs, past, FOX_W),
                        c_col8, c_meta_row, c_past, c_new, fox_o, geo)
    diff_o = _pair_attn_t(qd, kd, vd, (lam_p, diff_subln[0][:, None]), geo, False, lambda_init)
    diff_o = _diff_sample(qd, kd, vd, cache_diff_k[0].reshape(bs, past, 512), cache_diff_v[0].reshape(bs, past, 512),
                          lam_p, diff_subln[0][None, :], diff_o, geo, lambda_init)

    wo = w_out_even[0].astype(BF16)
    wr, br = _router_weights(moe_w_group[0], moe_b_group[0], moe_w_router[0], moe_b_router[0])
    x = _moe_layer([fox_o, diff_o], [wo[:FOX_W], wo[FOX_W:]], x, norm_ffn[0][None, :], wr, br,
                   moe_w_gate[0], moe_w_up[0], moe_w_down[0], norm_final[None, :], geo, False)

    w = w_in_odd[0]
    o = MLA_Q_RANK + MLA_KV_RANK
    wkr = w[:, o:o + MLA_ROPE]
    z64 = jnp.zeros((d, MLA_ROPE), F32)
    w_all = jnp.concatenate([w[:, :o], wkr, z64, _rot_cols(wkr, MLA_ROPE), z64], axis=1).astype(BF16)
    uq = mla_w_uq[0]
    uq_rope = uq[:, :, MLA_NOPE:]
    zr = jnp.zeros_like(uq_rope)
    wuq = jnp.concatenate([uq[:, :, :MLA_NOPE].reshape(MLA_Q_RANK, -1),
                           jnp.concatenate([uq_rope, zr], -1).reshape(MLA_Q_RANK, -1),
                           jnp.concatenate([_rot_cols(uq_rope, MLA_ROPE), zr], -1).reshape(MLA_Q_RANK, -1)],
                          axis=1).astype(BF16)
    wuk = mla_w_uk[0].transpose(1, 2, 0).astype(BF16)
    wuv = mla_w_uv[0].transpose(1, 0, 2).astype(BF16)
    wuvt = mla_w_uv[0].transpose(1, 2, 0).astype(BF16)
    qcat, kcat, kr32, ckvp, ckvs = _proj_odd(x, norm_mix[1][None, :], w_all, mla_norm_q[0][None, :],
                                             mla_norm_kv[0][None, :], wuq, wuk, cosp, sinp, geo)
    mla_o = _mla_prompt_t(qcat, kcat, wuvt, geo)
    mla_o = _mla_sample(qcat, kcat, cache_mla_ckv[0], cache_mla_krope[0], wuv, mla_o, geo)

    wr, br = _router_weights(moe_w_group[1], moe_b_group[1], moe_w_router[1], moe_b_router[1])
    y_prompt, y_sample = _moe_layer([mla_o], [w_out_odd[0].astype(BF16)], x, norm_ffn[1][None, :], wr, br,
                                    moe_w_gate[1], moe_w_up[1], moe_w_down[1], norm_final[None, :], geo, True)

    kr = kr32[:, :MLA_ROPE]
    kr_p = jnp.concatenate([jnp.broadcast_to(kr[M0:M0 + N_META][None], (bp, N_META, MLA_ROPE)),
                            kr[:PR].reshape(bp, lp, MLA_ROPE)], axis=1)
    return (y_prompt.reshape(bp, lp, d), y_sample.reshape(bs, ls, d),
            kfp.reshape(1, bp, L, FOX_HEADS, HEAD_DIM),
            vfp.reshape(1, bp, L, FOX_HEADS, HEAD_DIM),
            rows_p[None],
            kdp.reshape(1, bp, L, DIFF_HEADS, 2, HEAD_DIM),
            vdp.reshape(1, bp, L, DIFF_HEADS, DIFF_VDIM),
            ckvp[None], kr_p[None],
            kfs.reshape(1, bs, ls, FOX_HEADS, HEAD_DIM),
            vfs.reshape(1, bs, ls, FOX_HEADS, HEAD_DIM),
            lf8[PR:PR + SR].reshape(1, bs, ls, FOX_HEADS),
            kds.reshape(1, bs, ls, DIFF_HEADS, 2, HEAD_DIM),
            vds.reshape(1, bs, ls, DIFF_HEADS, DIFF_VDIM),
            ckvs.reshape(1, bs, ls, MLA_KV_RANK), kr[PR:PR + SR].reshape(1, bs, ls, MLA_ROPE))
```
